```python
import jax, jax.numpy as jnp
from jax import lax
import numpy as np

D_MODEL = 1024
BATCH = 8
SEQ = 4096
DEPTH = 1

D_MIX = D_MODEL
RWKV_WIDTH = D_MIX // 2
RWKV_HEAD = 64
RWKV_HEADS = RWKV_WIDTH // RWKV_HEAD
GMLP_WIDTH = D_MIX - RWKV_WIDTH
GMLP_GROUPS = 8
GMLP_GROUP_DIM = GMLP_WIDTH // GMLP_GROUPS
CHUNK = 128
DECAY_LORA = 64
AAA_LORA = 64
GATE_LORA = 128
D_FF = 2816
RMS_EPS = 1e-6
GN_EPS = 64e-5
LN_EPS = 1e-5
FFN_RES_SCALE = 0.5
D_SHIFT = 3 * RWKV_WIDTH + DECAY_LORA + AAA_LORA + GATE_LORA
D_IN = D_SHIFT + 2 * GMLP_WIDTH

kernel_name = "hymba_rwkv7_gmlp_macaron"


def rmsnorm(x, g):
    xf = x.astype(jnp.float32)
    y = xf * lax.rsqrt(jnp.mean(xf * xf, axis=-1, keepdims=True) + RMS_EPS)
    return (y * g.astype(jnp.float32)).astype(x.dtype)


def swiglu(h, w1, w3, w2):
    return (jax.nn.silu(h @ w1) * (h @ w3)) @ w2


def token_shift(p):
    return jnp.pad(p[:, :-1], ((0, 0), (1, 0), (0, 0)))


def rwkv7_scan(r, decay, k, v, a, b):
    seq_first = lambda t: jnp.transpose(t, (1, 0, 2, 3))
    B, _, H, N = r.shape

    def step(state, inp):
        r_t, w_t, k_t, v_t, a_t, b_t = inp
        sa = jnp.einsum('bhij,bhj->bhi', state, a_t)
        state = (state * w_t[:, :, None, :]
                 + sa[..., None] * b_t[:, :, None, :]
                 + v_t[..., None] * k_t[:, :, None, :])
        y_t = jnp.einsum('bhij,bhj->bhi', state, r_t)
        return state, y_t

    s0 = jnp.zeros((B, H, N, N), jnp.float32)
    _, ys = lax.scan(step, s0, tuple(seq_first(t) for t in (r, decay, k, v, a, b)))
    return jnp.transpose(ys, (1, 0, 2, 3))


def rwkv7_mixer(p, mu, w0, w_up, a0, a_up, g_up, k_k, k_a, r_k, gn_w, gn_b):
    B, S, _ = p.shape
    p = p + (token_shift(p) - p) * mu
    o1 = RWKV_WIDTH
    o2, o3 = 2 * o1, 3 * o1
    o4, o5 = o3 + DECAY_LORA, o3 + DECAY_LORA + AAA_LORA
    r, k, v = p[..., :o1], p[..., o1:o2], p[..., o2:o3]
    pw, pa, pg = p[..., o3:o4], p[..., o4:o5], p[..., o5:]
    w = -jax.nn.softplus(-(w0 + jnp.tanh(pw) @ w_up)) - 0.5
    decay = jnp.exp(-jnp.exp(w.astype(jnp.float32)))
    a = jax.nn.sigmoid(a0 + pa @ a_up)
    g = jax.nn.sigmoid(pg) @ g_up
    heads = lambda t: t.reshape(B, S, RWKV_HEADS, RWKV_HEAD).astype(jnp.float32)
    kk = heads(k * k_k)
    kk = kk / jnp.maximum(jnp.sqrt(jnp.sum(kk * kk, axis=-1, keepdims=True)), 1e-12)
    k = k * (1 + (a - 1) * k_a)
    rh, kh, vh, ah = heads(r), heads(k), heads(v), heads(a)
    y = rwkv7_scan(rh, heads(decay), kh, vh, -kk, kk * ah)
    mean = jnp.mean(y, axis=-1, keepdims=True)
    var = jnp.mean(jnp.square(y - mean), axis=-1, keepdims=True)
    y = (y - mean) * lax.rsqrt(var + GN_EPS)
    y = y * gn_w.reshape(RWKV_HEADS, RWKV_HEAD) + gn_b.reshape(RWKV_HEADS, RWKV_HEAD)
    bonus = jnp.sum(rh * kh * r_k.astype(jnp.float32), axis=-1, keepdims=True) * vh
    y = (y + bonus).reshape(B, S, RWKV_WIDTH).astype(p.dtype)
    return y * g


def gmlp_mixer(pu, pv, ln_g, ln_b, w_s, b_s):
    B, S, _ = pu.shape
    u = jax.nn.gelu(pu, approximate=False)
    vf = jax.nn.gelu(pv, approximate=False).astype(jnp.float32)
    mean = jnp.mean(vf, axis=-1, keepdims=True)
    var = jnp.mean(jnp.square(vf - mean), axis=-1, keepdims=True)
    vf = (vf - mean) * lax.rsqrt(var + LN_EPS) * ln_g + ln_b
    vc = vf.reshape(B, S // CHUNK, CHUNK, GMLP_GROUPS, GMLP_GROUP_DIM).astype(pu.dtype)
    w_causal = jnp.tril(w_s)
    mixed = jnp.einsum('gts,bcsgd->bctgd', w_causal, vc)
    mixed = mixed + jnp.transpose(b_s)[None, None, :, :, None]
    return u * mixed.reshape(B, S, GMLP_WIDTH)


def setup_inputs(seed: int = 0) -> dict:
    key = jax.random.key(seed)
    ks = jax.random.split(key, 32)
    f32 = jnp.float32
    nrm = lambda k, shape, s: jax.random.normal(k, shape, f32) * s
    L = DEPTH
    return {
        "x": jax.random.normal(ks[0], (BATCH, SEQ, D_MODEL), f32),
        "ffn1_norm": 1.0 + nrm(ks[1], (L, D_MODEL), 0.02),
        "ffn1_w1": nrm(ks[2], (L, D_MODEL, D_FF), D_MODEL ** -0.5),
        "ffn1_w3": nrm(ks[3], (L, D_MODEL, D_FF), D_MODEL ** -0.5),
        "ffn1_w2": nrm(ks[4], (L, D_FF, D_MODEL), D_FF ** -0.5),
        "mix_norm": 1.0 + nrm(ks[5], (L, D_MODEL), 0.02),
        "w_in": nrm(ks[6], (L, D_MODEL, D_IN), D_MODEL ** -0.5),
        "mu_shift": jax.random.uniform(ks[7], (L, D_SHIFT), f32),
        "w0": jax.random.uniform(ks[8], (L, RWKV_WIDTH), f32, -6.0, 0.0),
        "w_lora_up": nrm(ks[9], (L, DECAY_LORA, RWKV_WIDTH), 0.1 * DECAY_LORA ** -0.5),
        "a0": nrm(ks[10], (L, RWKV_WIDTH), 0.1),
        "a_lora_up": nrm(ks[11], (L, AAA_LORA, RWKV_WIDTH), 0.1 * AAA_LORA ** -0.5),
        "g_lora_up": nrm(ks[12], (L, GATE_LORA, RWKV_WIDTH), GATE_LORA ** -0.5),
        "k_k": 0.85 + nrm(ks[13], (L, RWKV_WIDTH), 0.02),
        "k_a": 1.0 + nrm(ks[14], (L, RWKV_WIDTH), 0.02),
        "r_k": nrm(ks[15], (L, RWKV_HEADS, RWKV_HEAD), 0.1),
        "gn_w": 1.0 + nrm(ks[16], (L, RWKV_WIDTH), 0.02),
        "gn_b": nrm(ks[17], (L, RWKV_WIDTH), 0.02),
        "sgu_ln_g": 1.0 + nrm(ks[18], (L, GMLP_WIDTH), 0.02),
        "sgu_ln_b": nrm(ks[19], (L, GMLP_WIDTH), 0.02),
        "sgu_w": nrm(ks[20], (L, GMLP_GROUPS, CHUNK, CHUNK), CHUNK ** -0.5),
        "sgu_b": 1.0 + nrm(ks[21], (L, GMLP_GROUPS, CHUNK), 0.02),
        "w_out": nrm(ks[22], (L, D_MIX, D_MODEL), D_MIX ** -0.5),
        "ffn2_norm": 1.0 + nrm(ks[23], (L, D_MODEL), 0.02),
        "ffn2_w1": nrm(ks[24], (L, D_MODEL, D_FF), D_MODEL ** -0.5),
        "ffn2_w3": nrm(ks[25], (L, D_MODEL, D_FF), D_MODEL ** -0.5),
        "ffn2_w2": nrm(ks[26], (L, D_FF, D_MODEL), D_FF ** -0.5),
        "final_norm": 1.0 + nrm(ks[27], (D_MODEL,), 0.02),
    }


def reference(x, ffn1_norm, ffn1_w1, ffn1_w3, ffn1_w2, mix_norm, w_in, mu_shift, w0, w_lora_up,
              a0, a_lora_up, g_lora_up, k_k, k_a, r_k, gn_w, gn_b, sgu_ln_g, sgu_ln_b, sgu_w,
              sgu_b, w_out, ffn2_norm, ffn2_w1, ffn2_w3, ffn2_w2, final_norm):
    for l in range(DEPTH):
        x = x + FFN_RES_SCALE * swiglu(rmsnorm(x, ffn1_norm[l]), ffn1_w1[l], ffn1_w3[l], ffn1_w2[l])
        h = rmsnorm(x, mix_norm[l])
        p = h @ w_in[l]
        y_rwkv = rwkv7_mixer(p[..., :D_SHIFT], mu_shift[l], w0[l], w_lora_up[l], a0[l],
                             a_lora_up[l], g_lora_up[l], k_k[l], k_a[l], r_k[l], gn_w[l], gn_b[l])
        y_gmlp = gmlp_mixer(p[..., D_SHIFT:D_SHIFT + GMLP_WIDTH], p[..., D_SHIFT + GMLP_WIDTH:],
                            sgu_ln_g[l], sgu_ln_b[l], sgu_w[l], sgu_b[l])
        x = x + jnp.concatenate([y_rwkv, y_gmlp], axis=-1) @ w_out[l]
        x = x + FFN_RES_SCALE * swiglu(rmsnorm(x, ffn2_norm[l]), ffn2_w1[l], ffn2_w3[l], ffn2_w2[l])
    return rmsnorm(x, final_norm)
```

```python
import functools
import math

import jax
import jax.numpy as jnp
from jax import lax
from jax.experimental import pallas as pl
from jax.experimental.pallas import tpu as pltpu

F32 = jnp.float32
BF16 = jnp.bfloat16

RWKV_WIDTH = 512
RWKV_HEAD = 64
GMLP_WIDTH = 512
GMLP_GROUPS = 8
CHUNK = 128
DECAY_LORA = 64
AAA_LORA = 64
GATE_LORA = 128
D_SHIFT = 3 * RWKV_WIDTH + DECAY_LORA + AAA_LORA + GATE_LORA
RMS_EPS = 1e-6
GN_EPS = 64e-5
LN_EPS = 1e-5
FFN_RES_SCALE = 0.5

LANES = 128
SCAN_CHUNK = 64
VMEM_LIMIT = 56 * 1024 * 1024

HI = lax.Precision.HIGHEST


def _rmsnorm(x, g):
    return x * lax.rsqrt(jnp.mean(x * x, axis=-1, keepdims=True) + RMS_EPS) * g


def _gelu(x):
    return 0.5 * x * (1.0 + lax.erf(x * (1.0 / math.sqrt(2.0))))


def _sigmoid(x):
    return 1.0 / (1.0 + jnp.exp(-x))


def _softplus(x):
    return jnp.maximum(x, 0.0) + jnp.log1p(jnp.exp(-jnp.abs(x)))


def _bdot(a, b):
    return jnp.dot(a.astype(BF16), b.astype(BF16), preferred_element_type=F32)


def _ffn_kernel(*refs, nf, mix, final):
    it = iter(refs)
    x_ref = next(it)
    if mix:
        yr_ref, yg_ref, wo_ref = next(it), next(it), next(it)
    g_ref, w1_ref, w3_ref, w2_ref = next(it), next(it), next(it), next(it)
    fg_ref = next(it) if final else None
    o_ref = next(it)
    xres_ref, hn_ref, acc_ref = next(it), next(it), next(it)

    f = pl.program_id(1)

    @pl.when(f == 0)
    def _():
        x = x_ref[...]
        if mix:
            x = x + jnp.dot(yr_ref[...], wo_ref[:RWKV_WIDTH, :], preferred_element_type=F32)
            x = x + jnp.dot(yg_ref[...], wo_ref[RWKV_WIDTH:, :], preferred_element_type=F32)
        xres_ref[...] = x
        hn_ref[...] = _rmsnorm(x, g_ref[...]).astype(BF16)

    h = hn_ref[...]
    a = jnp.dot(h, w1_ref[...], preferred_element_type=F32)
    b = jnp.dot(h, w3_ref[...], preferred_element_type=F32)
    z = (a * _sigmoid(a) * b).astype(BF16)
    part = jnp.dot(z, w2_ref[...], preferred_element_type=F32)

    @pl.when(f == 0)
    def _():
        acc_ref[...] = part

    @pl.when(f > 0)
    def _():
        acc_ref[...] += part

    @pl.when(f == nf - 1)
    def _():
        y = xres_ref[...] + FFN_RES_SCALE * acc_ref[...]
        if final:
            y = _rmsnorm(y, fg_ref[...])
        o_ref[...] = y


def _ffn(x, norm_g, w1, w3, w2, *, tm, tf, mix_in=None, final_g=None):
    T, D = x.shape
    F = w1.shape[1]
    nf = F // tf
    mix = mix_in is not None
    final = final_g is not None
    row = lambda i, f: (i, 0)
    const = lambda i, f: (0, 0)
    in_specs = [pl.BlockSpec((tm, D), row)]
    args = [x]
    if mix:
        yr, yg, wo = mix_in
        in_specs += [pl.BlockSpec((tm, yr.shape[1]), row), pl.BlockSpec((tm, yg.shape[1]), row),
                     pl.BlockSpec(wo.shape, const)]
        args += [yr, yg, wo]
    in_specs += [pl.BlockSpec((1, D), const),
                 pl.BlockSpec((D, tf), lambda i, f: (0, f)),
                 pl.BlockSpec((D, tf), lambda i, f: (0, f)),
                 pl.BlockSpec((tf, D), lambda i, f: (f, 0))]
    args += [norm_g, w1, w3, w2]
    if final:
        in_specs.append(pl.BlockSpec((1, D), const))
        args.append(final_g)
    return pl.pallas_call(
        functools.partial(_ffn_kernel, nf=nf, mix=mix, final=final),
        grid=(T // tm, nf),
        in_specs=in_specs,
        out_specs=pl.BlockSpec((tm, D), row),
        out_shape=jax.ShapeDtypeStruct((T, D), F32),
        scratch_shapes=[pltpu.VMEM((tm, D), F32), pltpu.VMEM((tm, D), BF16), pltpu.VMEM((tm, D), F32)],
        compiler_params=pltpu.CompilerParams(
            dimension_semantics=("arbitrary", "arbitrary"), vmem_limit_bytes=VMEM_LIMIT),
        name="ffn_mix_final" if mix else "ffn",
    )(*args)


def _head_sums(x, lane_lo):
    s0 = jnp.sum(jnp.where(lane_lo, x, 0.0), axis=-1, keepdims=True)
    s1 = jnp.sum(jnp.where(lane_lo, 0.0, x), axis=-1, keepdims=True)
    return jnp.where(lane_lo, s0, s1)


def _mixin_kernel(x_ref, g_ref, win_ref, mu_ref, w0_ref, a0_ref, lora_ref, gup_ref, kk_ref, ka_ref,
                  lng_ref, lnb_ref, sw_ref, sb_ref,
                  r_o, k_o, v_o, na_o, bb_o, lw_o, g_o, ygm_o, prev_ref, *, tm, tiles_per_seq):
    i = pl.program_id(0)
    W = RWKV_WIDTH
    h = _rmsnorm(x_ref[...], g_ref[...]).astype(BF16)
    p = jnp.dot(h, win_ref[...], preferred_element_type=F32)

    ps = p[:, :D_SHIFT]
    rolled = pltpu.roll(ps, 1, 0)
    first = (i % tiles_per_seq) == 0
    prev = jnp.where(first, 0.0, prev_ref[...])
    row = lax.broadcasted_iota(jnp.int32, (tm, 1), 0)
    shifted = jnp.where(row == 0, prev, rolled)
    prev_ref[...] = ps[tm - 1:tm, :]
    ps = ps + (shifted - ps) * mu_ref[...]

    r = ps[:, :W]
    k = ps[:, W:2 * W]
    v = ps[:, 2 * W:3 * W]
    slab = ps[:, 3 * W:3 * W + DECAY_LORA + AAA_LORA]
    pg = ps[:, 3 * W + DECAY_LORA + AAA_LORA:]
    lane = lax.broadcasted_iota(jnp.int32, (1, LANES), 1)
    slab = jnp.where(lane < DECAY_LORA, jnp.tanh(slab), slab)
    lora = _bdot(slab, lora_ref[...])
    w = -_softplus(-(w0_ref[...] + lora[:, :W])) - 0.5
    lw_o[...] = -jnp.exp(w)
    a = _sigmoid(a0_ref[...] + lora[:, W:])
    g_o[...] = _bdot(_sigmoid(pg), gup_ref[...])

    lane_lo = lane < RWKV_HEAD
    kk = k * kk_ref[...]
    for hp in range(W // LANES):
        sl = slice(hp * LANES, (hp + 1) * LANES)
        kkp = kk[:, sl]
        ss = _head_sums(kkp * kkp, lane_lo)
        kkn = kkp / jnp.maximum(jnp.sqrt(ss), 1e-12)
        na_o[:, sl] = -kkn
        bb_o[:, sl] = kkn * a[:, sl]
    r_o[...] = r
    k_o[...] = k * (1.0 + (a - 1.0) * ka_ref[...])
    v_o[...] = v

    u = _gelu(p[:, D_SHIFT:D_SHIFT + GMLP_WIDTH])
    vg = _gelu(p[:, D_SHIFT + GMLP_WIDTH:])
    mean = jnp.mean(vg, axis=-1, keepdims=True)
    d = vg - mean
    var = jnp.mean(d * d, axis=-1, keepdims=True)
    vn = (d * lax.rsqrt(var + LN_EPS) * lng_ref[...] + lnb_ref[...]).astype(BF16)
    tr = lax.broadcasted_iota(jnp.int32, (CHUNK, CHUNK), 0)
    tc = lax.broadcasted_iota(jnp.int32, (CHUNK, CHUNK), 1)
    ws = [jnp.where(tr >= tc, sw_ref[gi], 0.0).astype(BF16) for gi in range(GMLP_GROUPS)]
    bias = sb_ref[...]
    for c in range(tm // CHUNK):
        rows = slice(c * CHUNK, (c + 1) * CHUNK)
        outs = []
        for hp in range(GMLP_WIDTH // LANES):
            vp = vn[rows, hp * LANES:(hp + 1) * LANES]
            m0 = jnp.dot(ws[2 * hp], vp, preferred_element_type=F32)
            m1 = jnp.dot(ws[2 * hp + 1], vp, preferred_element_type=F32)
            outs.append(jnp.where(lane_lo, m0, m1))
        mixed = jnp.concatenate(outs, axis=1) + bias
        ygm_o[rows, :] = (u[rows, :] * mixed).astype(BF16)


def _mixin(x1, mix_g, w_in, mu, w0, a0, lora, g_up, k_k, k_a, ln_g, ln_b, sgu_w, sgu_bias, *, tm, seq):
    T, D = x1.shape
    W = RWKV_WIDTH
    row = lambda i: (i, 0)
    const = lambda i: (0, 0)
    full = lambda a: pl.BlockSpec(a.shape, (lambda i: (0,) * a.ndim))
    in_specs = [pl.BlockSpec((tm, D), row)] + [full(a) for a in
                (mix_g, w_in, mu, w0, a0, lora, g_up, k_k, k_a, ln_g, ln_b, sgu_w, sgu_bias)]
    out_shape = [jax.ShapeDtypeStruct((T, W), F32)] * 7 + [jax.ShapeDtypeStruct((T, GMLP_WIDTH), BF16)]
    out_specs = [pl.BlockSpec((tm, W), row)] * 7 + [pl.BlockSpec((tm, GMLP_WIDTH), row)]
    return pl.pallas_call(
        functools.partial(_mixin_kernel, tm=tm, tiles_per_seq=seq // tm),
        grid=(T // tm,),
        in_specs=in_specs,
        out_specs=out_specs,
        out_shape=out_shape,
        scratch_shapes=[pltpu.VMEM((1, D_SHIFT), F32)],
        compiler_params=pltpu.CompilerParams(
            dimension_semantics=("arbitrary",), vmem_limit_bytes=VMEM_LIMIT),
        name="mixin",
    )(x1, mix_g, w_in, mu, w0, a0, lora, g_up, k_k, k_a, ln_g, ln_b, sgu_w, sgu_bias)


def _hdot(a, b):
    return jnp.dot(a, b, preferred_element_type=F32, precision=HI)


def _hdot_nt(a, b):
    return lax.dot_general(a, b, (((1,), (1,)), ((), ())), preferred_element_type=F32, precision=HI)


def _hdot_tn(a, b):
    return lax.dot_general(a, b, (((0,), (0,)), ((), ())), preferred_element_type=F32, precision=HI)


def _scan_kernel(r_ref, k_ref, v_ref, na_ref, bb_ref, lw_ref, g_ref, gnw_ref, gnb_ref, rk_ref,
                 y_o, s_ref, yacc_ref, *, sb):
    C = SCAN_CHUNK
    P = 2 * C
    n_pairs = RWKV_WIDTH // LANES

    @pl.when(pl.program_id(1) == 0)
    def _():
        s_ref[...] = jnp.zeros_like(s_ref)

    lane = lax.broadcasted_iota(jnp.int32, (1, LANES), 1)
    lane_lo = lane < RWKV_HEAD
    ci = lax.broadcasted_iota(jnp.int32, (C, C), 0)
    cj = lax.broadcasted_iota(jnp.int32, (C, C), 1)
    tri = jnp.where(ci >= cj, 1.0, 0.0).astype(F32)
    pi = lax.broadcasted_iota(jnp.int32, (P, P), 0)
    pj = lax.broadcasted_iota(jnp.int32, (P, P), 1)
    same = (pi // C) == (pj // C)
    strict = jnp.logical_and(same, (pi % C) > (pj % C))
    incl = jnp.logical_and(same, (pi % C) >= (pj % C))
    eye = jnp.where(pi == pj, 1.0, 0.0).astype(F32)

    def stack(x):
        return jnp.concatenate([jnp.where(lane_lo, x, 0.0), jnp.where(lane_lo, 0.0, x)], axis=0)

    def chunk(c, carry):
        rows = pl.ds(pl.multiple_of(c * C, C), C)
        lw = lw_ref[rows, :]
        cum = _hdot(tri, lw)
        e_pos = jnp.exp(cum)
        e_neg = jnp.exp(-cum)
        e_prev = jnp.exp(cum - lw)
        e_end = jnp.exp(cum[C - 1:C, :] - cum)
        d_end = jnp.exp(cum[C - 1:C, :])
        rt = r_ref[rows, :] * e_pos
        at = na_ref[rows, :] * e_prev
        kh = k_ref[rows, :] * e_neg
        bh = bb_ref[rows, :] * e_neg
        ke = k_ref[rows, :] * e_end
        be = bb_ref[rows, :] * e_end
        vv = v_ref[rows, :]
        for hp in range(n_pairs):
            sl = slice(hp * LANES, (hp + 1) * LANES)
            xa, xr = stack(at[:, sl]), stack(rt[:, sl])
            yk, yb = stack(kh[:, sl]), stack(bh[:, sl])
            vs = stack(vv[:, sl])
            x2 = jnp.concatenate([xa, xr], axis=0)
            gram = _hdot_nt(x2, jnp.concatenate([yk, yb], axis=0))
            a_ak = jnp.where(strict, gram[:P, :P], 0.0)
            a_ab = jnp.where(strict, gram[:P, P:], 0.0)
            a_rk = jnp.where(incl, gram[P:, :P], 0.0)
            a_rb = jnp.where(incl, gram[P:, P:], 0.0)
            inv = eye + a_ab
            q = a_ab
            for _ in range(int(math.log2(C)) - 1):
                q = _hdot(q, q)
                inv = inv + _hdot(inv, q)
            s = s_ref[hp]
            xs = _hdot_nt(x2, s)
            av = _hdot(jnp.concatenate([a_ak, a_rk], axis=0), vs)
            u = _hdot(inv, xs[:P] + av[:P])
            ys = xs[P:] + av[P:] + _hdot(a_rb, u)
            yacc_ref[rows, sl] = ys[:C] + ys[C:]
            upd = _hdot_tn(jnp.concatenate([vs, u], axis=0),
                           jnp.concatenate([stack(ke[:, sl]), stack(be[:, sl])], axis=0))
            s_ref[hp] = s * d_end[:, sl] + upd
        return carry

    lax.fori_loop(0, sb // C, chunk, 0)

    for hp in range(n_pairs):
        sl = slice(hp * LANES, (hp + 1) * LANES)
        y = yacc_ref[:, sl]
        mean = _head_sums(y, lane_lo) * (1.0 / RWKV_HEAD)
        d = y - mean
        var = _head_sums(d * d, lane_lo) * (1.0 / RWKV_HEAD)
        yn = d * lax.rsqrt(var + GN_EPS) * gnw_ref[:, sl] + gnb_ref[:, sl]
        bonus = _head_sums(r_ref[:, sl] * k_ref[:, sl] * rk_ref[:, sl], lane_lo) * v_ref[:, sl]
        y_o[:, sl] = ((yn + bonus) * g_ref[:, sl]).astype(BF16)


def _scan(r, k, v, na, bb, lw, g, gn_w, gn_b, r_k, *, batch, seq, sb):
    T, W = r.shape
    nb = seq // sb
    row = lambda b, j: (b * nb + j, 0)
    const = lambda b, j: (0, 0)
    in_specs = [pl.BlockSpec((sb, W), row)] * 7 + [pl.BlockSpec((1, W), const)] * 3
    return pl.pallas_call(
        functools.partial(_scan_kernel, sb=sb),
        grid=(batch, nb),
        in_specs=in_specs,
        out_specs=pl.BlockSpec((sb, W), row),
        out_shape=jax.ShapeDtypeStruct((T, W), BF16),
        scratch_shapes=[pltpu.VMEM((W // LANES, LANES, LANES), F32), pltpu.VMEM((sb, W), F32)],
        compiler_params=pltpu.CompilerParams(
            dimension_semantics=("arbitrary", "arbitrary"), vmem_limit_bytes=VMEM_LIMIT),
        name="rwkv_scan",
    )(r, k, v, na, bb, lw, g, gn_w, gn_b, r_k)


def _tile(n, pref):
    return pref if n % pref == 0 else CHUNK


def kernel(x, ffn1_norm, ffn1_w1, ffn1_w3, ffn1_w2, mix_norm, w_in, mu_shift, w0, w_lora_up, a0, a_lora_up, g_lora_up, k_k, k_a, r_k, gn_w, gn_b, sgu_ln_g, sgu_ln_b, sgu_w, sgu_b, w_out, ffn2_norm, ffn2_w1, ffn2_w3, ffn2_w2, final_norm):
    B, S, D = x.shape
    T = B * S
    depth = ffn1_norm.shape[0]
    d_ff = ffn1_w1.shape[-1]
    tf = d_ff // 2 if (d_ff // 2) % LANES == 0 else d_ff
    tm_ffn = _tile(S, 512)
    tm_mix = _tile(S, 256)
    sb = _tile(S, 512)
    rowv = lambda a: a.reshape(1, -1).astype(F32)

    xt = x.reshape(T, D)
    for l in range(depth):
        xt = _ffn(xt, rowv(ffn1_norm[l]), ffn1_w1[l].astype(BF16), ffn1_w3[l].astype(BF16),
                  ffn1_w2[l].astype(BF16), tm=tm_ffn, tf=tf)
        zeros = jnp.zeros((DECAY_LORA, RWKV_WIDTH), F32)
        lora = jnp.concatenate([jnp.concatenate([w_lora_up[l], zeros], axis=1),
                                jnp.concatenate([zeros, a_lora_up[l]], axis=1)], axis=0)
        sgu_bias = jnp.repeat(jnp.transpose(sgu_b[l]), GMLP_WIDTH // GMLP_GROUPS, axis=1)
        r, k, v, na, bb, lw, g, y_gmlp = _mixin(
            xt, rowv(mix_norm[l]), w_in[l].astype(BF16), rowv(mu_shift[l]), rowv(w0[l]), rowv(a0[l]),
            lora.astype(BF16), g_lora_up[l].astype(BF16), rowv(k_k[l]), rowv(k_a[l]),
            rowv(sgu_ln_g[l]), rowv(sgu_ln_b[l]), sgu_w[l], sgu_bias, tm=tm_mix, seq=S)
        y_rwkv = _scan(r, k, v, na, bb, lw, g, rowv(gn_w[l]), rowv(gn_b[l]), rowv(r_k[l]),
                       batch=B, seq=S, sb=sb)
        last = l == depth - 1
        xt = _ffn(xt, rowv(ffn2_norm[l]), ffn2_w1[l].astype(BF16), ffn2_w3[l].astype(BF16),
                  ffn2_w2[l].astype(BF16), tm=tm_ffn, tf=tf,
                  mix_in=(y_rwkv, y_gmlp, w_out[l].astype(BF16)),
                  final_g=rowv(final_norm) if last else None)
    if depth == 0:
        raise ValueError("depth must be positive")
    return xt.reshape(B, S, D)
```

```python
import functools
import math

import jax
import jax.numpy as jnp
from jax import lax
from jax.experimental import pallas as pl
from jax.experimental.pallas import tpu as pltpu

F32 = jnp.float32
BF16 = jnp.bfloat16

RWKV_WIDTH = 512
RWKV_HEAD = 64
GMLP_WIDTH = 512
GMLP_GROUPS = 8
CHUNK = 128
DECAY_LORA = 64
AAA_LORA = 64
GATE_LORA = 128
D_SHIFT = 3 * RWKV_WIDTH + DECAY_LORA + AAA_LORA + GATE_LORA
RMS_EPS = 1e-6
GN_EPS = 64e-5
LN_EPS = 1e-5
FFN_RES_SCALE = 0.5

LANES = 128
SCAN_CHUNK = 64
VMEM_LIMIT = 56 * 1024 * 1024


def _rmsnorm(x, g):
    return x * lax.rsqrt(jnp.mean(x * x, axis=-1, keepdims=True) + RMS_EPS) * g


def _gelu(x):
    return 0.5 * x * (1.0 + lax.erf(x * (1.0 / math.sqrt(2.0))))


def _sigmoid(x):
    return 1.0 / (1.0 + jnp.exp(-x))


def _softplus(x):
    return jnp.maximum(x, 0.0) + jnp.log1p(jnp.exp(-jnp.abs(x)))


def _bdot(a, b):
    return jnp.dot(a.astype(BF16), b.astype(BF16), preferred_element_type=F32)


def _ffn_kernel(*refs, nf, mix, final):
    it = iter(refs)
    x_ref = next(it)
    if mix:
        yr_ref, yg_ref, wo_ref = next(it), next(it), next(it)
    g_ref, w1_ref, w3_ref, w2_ref = next(it), next(it), next(it), next(it)
    fg_ref = next(it) if final else None
    o_ref = next(it)
    xres_ref, hn_ref, acc_ref = next(it), next(it), next(it)

    f = pl.program_id(1)

    @pl.when(f == 0)
    def _():
        x = x_ref[...]
        if mix:
            x = x + jnp.dot(yr_ref[...], wo_ref[:RWKV_WIDTH, :], preferred_element_type=F32)
            x = x + jnp.dot(yg_ref[...], wo_ref[RWKV_WIDTH:, :], preferred_element_type=F32)
        xres_ref[...] = x
        hn_ref[...] = _rmsnorm(x, g_ref[...]).astype(BF16)

    h = hn_ref[...]
    a = jnp.dot(h, w1_ref[...], preferred_element_type=F32)
    b = jnp.dot(h, w3_ref[...], preferred_element_type=F32)
    z = (a * _sigmoid(a) * b).astype(BF16)
    part = jnp.dot(z, w2_ref[...], preferred_element_type=F32)

    @pl.when(f == 0)
    def _():
        acc_ref[...] = part

    @pl.when(f > 0)
    def _():
        acc_ref[...] += part

    @pl.when(f == nf - 1)
    def _():
        y = xres_ref[...] + FFN_RES_SCALE * acc_ref[...]
        if final:
            y = _rmsnorm(y, fg_ref[...])
        o_ref[...] = y


def _ffn(x, norm_g, w1, w3, w2, *, tm, tf, mix_in=None, final_g=None):
    T, D = x.shape
    F = w1.shape[1]
    nf = F // tf
    mix = mix_in is not None
    final = final_g is not None
    row = lambda i, f: (i, 0)
    const = lambda i, f: (0, 0)
    in_specs = [pl.BlockSpec((tm, D), row)]
    args = [x]
    if mix:
        yr, yg, wo = mix_in
        in_specs += [pl.BlockSpec((tm, yr.shape[1]), row), pl.BlockSpec((tm, yg.shape[1]), row),
                     pl.BlockSpec(wo.shape, const)]
        args += [yr, yg, wo]
    in_specs += [pl.BlockSpec((1, D), const),
                 pl.BlockSpec((D, tf), lambda i, f: (0, f)),
                 pl.BlockSpec((D, tf), lambda i, f: (0, f)),
                 pl.BlockSpec((tf, D), lambda i, f: (f, 0))]
    args += [norm_g, w1, w3, w2]
    if final:
        in_specs.append(pl.BlockSpec((1, D), const))
        args.append(final_g)
    return pl.pallas_call(
        functools.partial(_ffn_kernel, nf=nf, mix=mix, final=final),
        grid=(T // tm, nf),
        in_specs=in_specs,
        out_specs=pl.BlockSpec((tm, D), row),
        out_shape=jax.ShapeDtypeStruct((T, D), F32),
        scratch_shapes=[pltpu.VMEM((tm, D), F32), pltpu.VMEM((tm, D), BF16), pltpu.VMEM((tm, D), F32)],
        compiler_params=pltpu.CompilerParams(
            dimension_semantics=("arbitrary", "arbitrary"), vmem_limit_bytes=VMEM_LIMIT),
        name="ffn_mix_final" if mix else "ffn",
    )(*args)


def _head_sums(x, lane_lo):
    s0 = jnp.sum(jnp.where(lane_lo, x, 0.0), axis=-1, keepdims=True)
    s1 = jnp.sum(jnp.where(lane_lo, 0.0, x), axis=-1, keepdims=True)
    return jnp.where(lane_lo, s0, s1)


def _mixin_kernel(x_ref, g_ref, win_ref, mu_ref, w0_ref, a0_ref, lora_ref, gup_ref, kk_ref, ka_ref,
                  lng_ref, lnb_ref, sw_ref, sb_ref,
                  r_o, k_o, v_o, na_o, bb_o, lw_o, g_o, ygm_o, prev_ref, *, tm, tiles_per_seq):
    i = pl.program_id(0)
    W = RWKV_WIDTH
    h = _rmsnorm(x_ref[...], g_ref[...]).astype(BF16)
    p = jnp.dot(h, win_ref[...], preferred_element_type=F32)

    ps = p[:, :D_SHIFT]
    rolled = pltpu.roll(ps, 1, 0)
    first = (i % tiles_per_seq) == 0
    prev = jnp.where(first, 0.0, prev_ref[...])
    row = lax.broadcasted_iota(jnp.int32, (tm, 1), 0)
    shifted = jnp.where(row == 0, prev, rolled)
    prev_ref[...] = ps[tm - 1:tm, :]
    ps = ps + (shifted - ps) * mu_ref[...]

    r = ps[:, :W]
    k = ps[:, W:2 * W]
    v = ps[:, 2 * W:3 * W]
    slab = ps[:, 3 * W:3 * W + DECAY_LORA + AAA_LORA]
    pg = ps[:, 3 * W + DECAY_LORA + AAA_LORA:]
    lane = lax.broadcasted_iota(jnp.int32, (1, LANES), 1)
    slab = jnp.where(lane < DECAY_LORA, jnp.tanh(slab), slab)
    lora = _bdot(slab, lora_ref[...])
    w = -_softplus(-(w0_ref[...] + lora[:, :W])) - 0.5
    lw_o[...] = -jnp.exp(w)
    a = _sigmoid(a0_ref[...] + lora[:, W:])
    g_o[...] = _bdot(_sigmoid(pg), gup_ref[...])

    lane_lo = lane < RWKV_HEAD
    kk = k * kk_ref[...]
    for hp in range(W // LANES):
        sl = slice(hp * LANES, (hp + 1) * LANES)
        kkp = kk[:, sl]
        ss = _head_sums(kkp * kkp, lane_lo)
        kkn = kkp / jnp.maximum(jnp.sqrt(ss), 1e-12)
        na_o[:, sl] = -kkn
        bb_o[:, sl] = kkn * a[:, sl]
    r_o[...] = r
    k_o[...] = k * (1.0 + (a - 1.0) * ka_ref[...])
    v_o[...] = v

    u = _gelu(p[:, D_SHIFT:D_SHIFT + GMLP_WIDTH])
    vg = _gelu(p[:, D_SHIFT + GMLP_WIDTH:])
    mean = jnp.mean(vg, axis=-1, keepdims=True)
    d = vg - mean
    var = jnp.mean(d * d, axis=-1, keepdims=True)
    vn = (d * lax.rsqrt(var + LN_EPS) * lng_ref[...] + lnb_ref[...]).astype(BF16)
    tr = lax.broadcasted_iota(jnp.int32, (CHUNK, CHUNK), 0)
    tc = lax.broadcasted_iota(jnp.int32, (CHUNK, CHUNK), 1)
    ws = [jnp.where(tr >= tc, sw_ref[gi], 0.0).astype(BF16) for gi in range(GMLP_GROUPS)]
    bias = sb_ref[...]
    for c in range(tm // CHUNK):
        rows = slice(c * CHUNK, (c + 1) * CHUNK)
        outs = []
        for hp in range(GMLP_WIDTH // LANES):
            vp = vn[rows, hp * LANES:(hp + 1) * LANES]
            m0 = jnp.dot(ws[2 * hp], vp, preferred_element_type=F32)
            m1 = jnp.dot(ws[2 * hp + 1], vp, preferred_element_type=F32)
            outs.append(jnp.where(lane_lo, m0, m1))
        mixed = jnp.concatenate(outs, axis=1) + bias
        ygm_o[rows, :] = (u[rows, :] * mixed).astype(BF16)


def _mixin(x1, mix_g, w_in, mu, w0, a0, lora, g_up, k_k, k_a, ln_g, ln_b, sgu_w, sgu_bias, *, tm, seq):
    T, D = x1.shape
    W = RWKV_WIDTH
    row = lambda i: (i, 0)
    const = lambda i: (0, 0)
    full = lambda a: pl.BlockSpec(a.shape, (lambda i: (0,) * a.ndim))
    in_specs = [pl.BlockSpec((tm, D), row)] + [full(a) for a in
                (mix_g, w_in, mu, w0, a0, lora, g_up, k_k, k_a, ln_g, ln_b, sgu_w, sgu_bias)]
    out_shape = [jax.ShapeDtypeStruct((T, W), F32)] * 7 + [jax.ShapeDtypeStruct((T, GMLP_WIDTH), BF16)]
    out_specs = [pl.BlockSpec((tm, W), row)] * 7 + [pl.BlockSpec((tm, GMLP_WIDTH), row)]
    return pl.pallas_call(
        functools.partial(_mixin_kernel, tm=tm, tiles_per_seq=seq // tm),
        grid=(T // tm,),
        in_specs=in_specs,
        out_specs=out_specs,
        out_shape=out_shape,
        scratch_shapes=[pltpu.VMEM((1, D_SHIFT), F32)],
        compiler_params=pltpu.CompilerParams(
            dimension_semantics=("arbitrary",), vmem_limit_bytes=VMEM_LIMIT),
        name="mixin",
    )(x1, mix_g, w_in, mu, w0, a0, lora, g_up, k_k, k_a, ln_g, ln_b, sgu_w, sgu_bias)


def _dot_nn(a, b):
    return jnp.dot(a, b, preferred_element_type=F32)


def _dot_nt(a, b):
    return lax.dot_general(a, b, (((1,), (1,)), ((), ())), preferred_element_type=F32)


def _dot_tn(a, b):
    return lax.dot_general(a, b, (((0,), (0,)), ((), ())), preferred_element_type=F32)


def _cumsum_rows(tri, x):
    out = None
    for _ in range(3):
        hi = x.astype(BF16)
        t = _dot_nn(tri, hi)
        out = t if out is None else out + t
        x = x - hi.astype(F32)
    return out


def _scan_kernel(r_ref, k_ref, v_ref, na_ref, bb_ref, lw_ref, g_ref, gnw_ref, gnb_ref, rk_ref,
                 y_o, s_ref, yacc_ref, *, sb):
    C = SCAN_CHUNK
    P = 2 * C
    n_pairs = RWKV_WIDTH // LANES

    @pl.when(pl.program_id(1) == 0)
    def _():
        s_ref[...] = jnp.zeros_like(s_ref)

    lane = lax.broadcasted_iota(jnp.int32, (1, LANES), 1)
    lane_lo = lane < RWKV_HEAD
    ci = lax.broadcasted_iota(jnp.int32, (C, C), 0)
    cj = lax.broadcasted_iota(jnp.int32, (C, C), 1)
    tri = jnp.where(ci >= cj, 1.0, 0.0).astype(BF16)
    pi = lax.broadcasted_iota(jnp.int32, (P, P), 0)
    pj = lax.broadcasted_iota(jnp.int32, (P, P), 1)
    same = (pi // C) == (pj // C)
    strict = jnp.logical_and(same, (pi % C) > (pj % C))
    incl = jnp.logical_and(same, (pi % C) >= (pj % C))
    eye = jnp.where(pi == pj, 1.0, 0.0).astype(F32)

    def stack(x):
        return jnp.concatenate([jnp.where(lane_lo, x, 0.0), jnp.where(lane_lo, 0.0, x)], axis=0)

    def chunk(c, carry):
        rows = pl.ds(pl.multiple_of(c * C, C), C)
        lw = lw_ref[rows, :]
        cum = _cumsum_rows(tri, lw)
        e_pos = jnp.exp(cum)
        e_neg = jnp.exp(-cum)
        d_end = e_pos[C - 1:C, :]
        rt = r_ref[rows, :] * e_pos
        at = na_ref[rows, :] * jnp.exp(cum - lw)
        kh = k_ref[rows, :] * e_neg
        bh = bb_ref[rows, :] * e_neg
        vv = v_ref[rows, :]
        hps = range(n_pairs)
        sls = [slice(hp * LANES, (hp + 1) * LANES) for hp in hps]
        x2 = [jnp.concatenate([stack(at[:, sl]), stack(rt[:, sl])], axis=0).astype(BF16) for sl in sls]
        y2 = [jnp.concatenate([stack(kh[:, sl]), stack(bh[:, sl])], axis=0).astype(BF16) for sl in sls]
        vs = [stack(vv[:, sl]).astype(BF16) for sl in sls]
        gram = [_dot_nt(x2[h], y2[h]) for h in hps]
        s = [s_ref[h] for h in hps]
        xs = [_dot_nt(x2[h], s[h].astype(BF16)) for h in hps]
        a_ab = [jnp.where(strict, gram[h][:P, P:], 0.0) for h in hps]
        a_kr = [jnp.concatenate([jnp.where(strict, gram[h][:P, :P], 0.0),
                                 jnp.where(incl, gram[h][P:, :P], 0.0)], axis=0).astype(BF16) for h in hps]
        a_rb = [jnp.where(incl, gram[h][P:, P:], 0.0).astype(BF16) for h in hps]
        av = [_dot_nn(a_kr[h], vs[h]) for h in hps]
        inv = [eye + a_ab[h] for h in hps]
        q = [a_ab[h].astype(BF16) for h in hps]
        for _ in range(int(math.log2(C)) - 1):
            q = [_dot_nn(q[h], q[h]).astype(BF16) for h in hps]
            inv = [inv[h] + _dot_nn(inv[h].astype(BF16), q[h]) for h in hps]
        u = [_dot_nn(inv[h].astype(BF16), (xs[h][:P] + av[h][:P]).astype(BF16)).astype(BF16) for h in hps]
        yu = [_dot_nn(a_rb[h], u[h]) for h in hps]
        upd = [_dot_tn(jnp.concatenate([vs[h], u[h]], axis=0), y2[h]) for h in hps]
        for h in hps:
            ys = xs[h][P:] + av[h][P:] + yu[h]
            yacc_ref[rows, sls[h]] = ys[:C] + ys[C:]
            s_ref[h] = s[h] * d_end[:, sls[h]] + upd[h] * d_end[:, sls[h]]
        return carry

    lax.fori_loop(0, sb // C, chunk, 0)

    for hp in range(n_pairs):
        sl = slice(hp * LANES, (hp + 1) * LANES)
        y = yacc_ref[:, sl]
        mean = _head_sums(y, lane_lo) * (1.0 / RWKV_HEAD)
        d = y - mean
        var = _head_sums(d * d, lane_lo) * (1.0 / RWKV_HEAD)
        yn = d * lax.rsqrt(var + GN_EPS) * gnw_ref[:, sl] + gnb_ref[:, sl]
        bonus = _head_sums(r_ref[:, sl] * k_ref[:, sl] * rk_ref[:, sl], lane_lo) * v_ref[:, sl]
        y_o[:, sl] = ((yn + bonus) * g_ref[:, sl]).astype(BF16)


def _scan(r, k, v, na, bb, lw, g, gn_w, gn_b, r_k, *, batch, seq, sb):
    T, W = r.shape
    nb = seq // sb
    row = lambda b, j: (b * nb + j, 0)
    const = lambda b, j: (0, 0)
    in_specs = [pl.BlockSpec((sb, W), row)] * 7 + [pl.BlockSpec((1, W), const)] * 3
    return pl.pallas_call(
        functools.partial(_scan_kernel, sb=sb),
        grid=(batch, nb),
        in_specs=in_specs,
        out_specs=pl.BlockSpec((sb, W), row),
        out_shape=jax.ShapeDtypeStruct((T, W), BF16),
        scratch_shapes=[pltpu.VMEM((W // LANES, LANES, LANES), F32), pltpu.VMEM((sb, W), F32)],
        compiler_params=pltpu.CompilerParams(
            dimension_semantics=("arbitrary", "arbitrary"), vmem_limit_bytes=VMEM_LIMIT),
        name="rwkv_scan",
    )(r, k, v, na, bb, lw, g, gn_w, gn_b, r_k)


def _tile(n, pref):
    return pref if n % pref == 0 else CHUNK


def kernel(x, ffn1_norm, ffn1_w1, ffn1_w3, ffn1_w2, mix_norm, w_in, mu_shift, w0, w_lora_up, a0, a_lora_up, g_lora_up, k_k, k_a, r_k, gn_w, gn_b, sgu_ln_g, sgu_ln_b, sgu_w, sgu_b, w_out, ffn2_norm, ffn2_w1, ffn2_w3, ffn2_w2, final_norm):
    B, S, D = x.shape
    T = B * S
    depth = ffn1_norm.shape[0]
    d_ff = ffn1_w1.shape[-1]
    tf = d_ff // 2 if (d_ff // 2) % LANES == 0 else d_ff
    tm_ffn = _tile(S, 512)
    tm_mix = _tile(S, 256)
    sb = _tile(S, 512)
    rowv = lambda a: a.reshape(1, -1).astype(F32)

    xt = x.reshape(T, D)
    for l in range(depth):
        xt = _ffn(xt, rowv(ffn1_norm[l]), ffn1_w1[l].astype(BF16), ffn1_w3[l].astype(BF16),
                  ffn1_w2[l].astype(BF16), tm=tm_ffn, tf=tf)
        zeros = jnp.zeros((DECAY_LORA, RWKV_WIDTH), F32)
        lora = jnp.concatenate([jnp.concatenate([w_lora_up[l], zeros], axis=1),
                                jnp.concatenate([zeros, a_lora_up[l]], axis=1)], axis=0)
        sgu_bias = jnp.repeat(jnp.transpose(sgu_b[l]), GMLP_WIDTH // GMLP_GROUPS, axis=1)
        r, k, v, na, bb, lw, g, y_gmlp = _mixin(
            xt, rowv(mix_norm[l]), w_in[l].astype(BF16), rowv(mu_shift[l]), rowv(w0[l]), rowv(a0[l]),
            lora.astype(BF16), g_lora_up[l].astype(BF16), rowv(k_k[l]), rowv(k_a[l]),
            rowv(sgu_ln_g[l]), rowv(sgu_ln_b[l]), sgu_w[l], sgu_bias, tm=tm_mix, seq=S)
        y_rwkv = _scan(r, k, v, na, bb, lw, g, rowv(gn_w[l]), rowv(gn_b[l]), rowv(r_k[l]),
                       batch=B, seq=S, sb=sb)
        last = l == depth - 1
        xt = _ffn(xt, rowv(ffn2_norm[l]), ffn2_w1[l].astype(BF16), ffn2_w3[l].astype(BF16),
                  ffn2_w2[l].astype(BF16), tm=tm_ffn, tf=tf,
                  mix_in=(y_rwkv, y_gmlp, w_out[l].astype(BF16)),
                  final_g=rowv(final_norm) if last else None)
    if depth == 0:
        raise ValueError("depth must be positive")
    return xt.reshape(B, S, D)
```

```python
import functools
import math

import jax
import jax.numpy as jnp
from jax import lax
from jax.experimental import pallas as pl
from jax.experimental.pallas import tpu as pltpu

F32 = jnp.float32
BF16 = jnp.bfloat16

RWKV_WIDTH = 512
RWKV_HEAD = 64
GMLP_WIDTH = 512
GMLP_GROUPS = 8
CHUNK = 128
DECAY_LORA = 64
AAA_LORA = 64
GATE_LORA = 128
D_SHIFT = 3 * RWKV_WIDTH + DECAY_LORA + AAA_LORA + GATE_LORA
RMS_EPS = 1e-6
GN_EPS = 64e-5
LN_EPS = 1e-5
FFN_RES_SCALE = 0.5

LANES = 128
SCAN_CHUNK = 64
SCAN_BATCH = 4
VMEM_LIMIT = 56 * 1024 * 1024


def _rmsnorm(x, g):
    return x * lax.rsqrt(jnp.mean(x * x, axis=-1, keepdims=True) + RMS_EPS) * g


def _gelu(x):
    return 0.5 * x * (1.0 + lax.erf(x * (1.0 / math.sqrt(2.0))))


def _sigmoid(x):
    return 1.0 / (1.0 + jnp.exp(-x))


def _softplus(x):
    return jnp.maximum(x, 0.0) + jnp.log1p(jnp.exp(-jnp.abs(x)))


def _bdot(a, b):
    return jnp.dot(a.astype(BF16), b.astype(BF16), preferred_element_type=F32)


def _ffn_kernel(*refs, nf, mix, final):
    it = iter(refs)
    x_ref = next(it)
    if mix:
        yr_ref, yg_ref, wo_ref = next(it), next(it), next(it)
    g_ref, w1_ref, w3_ref, w2_ref = next(it), next(it), next(it), next(it)
    fg_ref = next(it) if final else None
    o_ref = next(it)
    xres_ref, hn_ref, acc_ref = next(it), next(it), next(it)

    f = pl.program_id(1)

    @pl.when(f == 0)
    def _():
        x = x_ref[...]
        if mix:
            x = x + jnp.dot(yr_ref[...], wo_ref[:RWKV_WIDTH, :], preferred_element_type=F32)
            x = x + jnp.dot(yg_ref[...], wo_ref[RWKV_WIDTH:, :], preferred_element_type=F32)
        xres_ref[...] = x
        hn_ref[...] = _rmsnorm(x, g_ref[...]).astype(BF16)

    h = hn_ref[...]
    a = jnp.dot(h, w1_ref[...], preferred_element_type=F32)
    b = jnp.dot(h, w3_ref[...], preferred_element_type=F32)
    z = (a * _sigmoid(a) * b).astype(BF16)
    part = jnp.dot(z, w2_ref[...], preferred_element_type=F32)

    @pl.when(f == 0)
    def _():
        acc_ref[...] = part

    @pl.when(f > 0)
    def _():
        acc_ref[...] += part

    @pl.when(f == nf - 1)
    def _():
        y = xres_ref[...] + FFN_RES_SCALE * acc_ref[...]
        if final:
            y = _rmsnorm(y, fg_ref[...])
        o_ref[...] = y


def _ffn(x, norm_g, w1, w3, w2, *, tm, tf, mix_in=None, final_g=None):
    T, D = x.shape
    F = w1.shape[1]
    nf = F // tf
    mix = mix_in is not None
    final = final_g is not None
    row = lambda i, f: (i, 0)
    const = lambda i, f: (0, 0)
    in_specs = [pl.BlockSpec((tm, D), row)]
    args = [x]
    if mix:
        yr, yg, wo = mix_in
        in_specs += [pl.BlockSpec((tm, yr.shape[1]), row), pl.BlockSpec((tm, yg.shape[1]), row),
                     pl.BlockSpec(wo.shape, const)]
        args += [yr, yg, wo]
    in_specs += [pl.BlockSpec((1, D), const),
                 pl.BlockSpec((D, tf), lambda i, f: (0, f)),
                 pl.BlockSpec((D, tf), lambda i, f: (0, f)),
                 pl.BlockSpec((tf, D), lambda i, f: (f, 0))]
    args += [norm_g, w1, w3, w2]
    if final:
        in_specs.append(pl.BlockSpec((1, D), const))
        args.append(final_g)
    return pl.pallas_call(
        functools.partial(_ffn_kernel, nf=nf, mix=mix, final=final),
        grid=(T // tm, nf),
        in_specs=in_specs,
        out_specs=pl.BlockSpec((tm, D), row),
        out_shape=jax.ShapeDtypeStruct((T, D), F32),
        scratch_shapes=[pltpu.VMEM((tm, D), F32), pltpu.VMEM((tm, D), BF16), pltpu.VMEM((tm, D), F32)],
        compiler_params=pltpu.CompilerParams(
            dimension_semantics=("arbitrary", "arbitrary"), vmem_limit_bytes=VMEM_LIMIT),
        name="ffn_mix_final" if mix else "ffn",
    )(*args)


def _head_sums(x, lane_lo):
    s0 = jnp.sum(jnp.where(lane_lo, x, 0.0), axis=-1, keepdims=True)
    s1 = jnp.sum(jnp.where(lane_lo, 0.0, x), axis=-1, keepdims=True)
    return jnp.where(lane_lo, s0, s1)


def _mixin_kernel(x_ref, g_ref, win_ref, mu_ref, w0_ref, a0_ref, lora_ref, gup_ref, kk_ref, ka_ref,
                  lng_ref, lnb_ref, sw_ref, sb_ref,
                  r_o, k_o, v_o, na_o, bb_o, lw_o, g_o, ygm_o, prev_ref, *, tm, tiles_per_seq):
    i = pl.program_id(0)
    W = RWKV_WIDTH
    h = _rmsnorm(x_ref[...], g_ref[...]).astype(BF16)
    p = jnp.dot(h, win_ref[...], preferred_element_type=F32)

    ps = p[:, :D_SHIFT]
    rolled = pltpu.roll(ps, 1, 0)
    first = (i % tiles_per_seq) == 0
    prev = jnp.where(first, 0.0, prev_ref[...])
    row = lax.broadcasted_iota(jnp.int32, (tm, 1), 0)
    shifted = jnp.where(row == 0, prev, rolled)
    prev_ref[...] = ps[tm - 1:tm, :]
    ps = ps + (shifted - ps) * mu_ref[...]

    r = ps[:, :W]
    k = ps[:, W:2 * W]
    v = ps[:, 2 * W:3 * W]
    slab = ps[:, 3 * W:3 * W + DECAY_LORA + AAA_LORA]
    pg = ps[:, 3 * W + DECAY_LORA + AAA_LORA:]
    lane = lax.broadcasted_iota(jnp.int32, (1, LANES), 1)
    slab = jnp.where(lane < DECAY_LORA, jnp.tanh(slab), slab)
    lora = _bdot(slab, lora_ref[...])
    w = -_softplus(-(w0_ref[...] + lora[:, :W])) - 0.5
    lw_o[...] = -jnp.exp(w)
    a = _sigmoid(a0_ref[...] + lora[:, W:])
    g_o[...] = _bdot(_sigmoid(pg), gup_ref[...])

    lane_lo = lane < RWKV_HEAD
    kk = k * kk_ref[...]
    for hp in range(W // LANES):
        sl = slice(hp * LANES, (hp + 1) * LANES)
        kkp = kk[:, sl]
        ss = _head_sums(kkp * kkp, lane_lo)
        kkn = kkp / jnp.maximum(jnp.sqrt(ss), 1e-12)
        na_o[:, sl] = -kkn
        bb_o[:, sl] = kkn * a[:, sl]
    r_o[...] = r
    k_o[...] = k * (1.0 + (a - 1.0) * ka_ref[...])
    v_o[...] = v

    u = _gelu(p[:, D_SHIFT:D_SHIFT + GMLP_WIDTH])
    vg = _gelu(p[:, D_SHIFT + GMLP_WIDTH:])
    mean = jnp.mean(vg, axis=-1, keepdims=True)
    d = vg - mean
    var = jnp.mean(d * d, axis=-1, keepdims=True)
    vn = (d * lax.rsqrt(var + LN_EPS) * lng_ref[...] + lnb_ref[...]).astype(BF16)
    tr = lax.broadcasted_iota(jnp.int32, (CHUNK, CHUNK), 0)
    tc = lax.broadcasted_iota(jnp.int32, (CHUNK, CHUNK), 1)
    ws = [jnp.where(tr >= tc, sw_ref[gi], 0.0).astype(BF16) for gi in range(GMLP_GROUPS)]
    bias = sb_ref[...]
    for c in range(tm // CHUNK):
        rows = slice(c * CHUNK, (c + 1) * CHUNK)
        outs = []
        for hp in range(GMLP_WIDTH // LANES):
            vp = vn[rows, hp * LANES:(hp + 1) * LANES]
            m0 = jnp.dot(ws[2 * hp], vp, preferred_element_type=F32)
            m1 = jnp.dot(ws[2 * hp + 1], vp, preferred_element_type=F32)
            outs.append(jnp.where(lane_lo, m0, m1))
        mixed = jnp.concatenate(outs, axis=1) + bias
        ygm_o[rows, :] = (u[rows, :] * mixed).astype(BF16)


def _mixin(x1, mix_g, w_in, mu, w0, a0, lora, g_up, k_k, k_a, ln_g, ln_b, sgu_w, sgu_bias, *, tm, seq):
    T, D = x1.shape
    W = RWKV_WIDTH
    row = lambda i: (i, 0)
    const = lambda i: (0, 0)
    full = lambda a: pl.BlockSpec(a.shape, (lambda i: (0,) * a.ndim))
    in_specs = [pl.BlockSpec((tm, D), row)] + [full(a) for a in
                (mix_g, w_in, mu, w0, a0, lora, g_up, k_k, k_a, ln_g, ln_b, sgu_w, sgu_bias)]
    out_shape = [jax.ShapeDtypeStruct((T, W), F32)] * 7 + [jax.ShapeDtypeStruct((T, GMLP_WIDTH), BF16)]
    out_specs = [pl.BlockSpec((tm, W), row)] * 7 + [pl.BlockSpec((tm, GMLP_WIDTH), row)]
    return pl.pallas_call(
        functools.partial(_mixin_kernel, tm=tm, tiles_per_seq=seq // tm),
        grid=(T // tm,),
        in_specs=in_specs,
        out_specs=out_specs,
        out_shape=out_shape,
        scratch_shapes=[pltpu.VMEM((1, D_SHIFT), F32)],
        compiler_params=pltpu.CompilerParams(
            dimension_semantics=("arbitrary",), vmem_limit_bytes=VMEM_LIMIT),
        name="mixin",
    )(x1, mix_g, w_in, mu, w0, a0, lora, g_up, k_k, k_a, ln_g, ln_b, sgu_w, sgu_bias)


def _dot_nn(a, b):
    return jnp.dot(a, b, preferred_element_type=F32)


def _dot_nt(a, b):
    return lax.dot_general(a, b, (((1,), (1,)), ((), ())), preferred_element_type=F32)


def _dot_tn(a, b):
    return lax.dot_general(a, b, (((0,), (0,)), ((), ())), preferred_element_type=F32)


def _cumsum_rows(tri, x):
    out = None
    for _ in range(3):
        hi = x.astype(BF16)
        t = _dot_nn(tri, hi)
        out = t if out is None else out + t
        x = x - hi.astype(F32)
    return out


def _scan_kernel(r_ref, k_ref, v_ref, na_ref, bb_ref, lw_ref, g_ref, gnw_ref, gnb_ref, rk_ref,
                 y_o, s_ref, yacc_ref, *, sb, nbat):
    C = SCAN_CHUNK
    P = 2 * C
    n_pairs = RWKV_WIDTH // LANES
    sls = [slice(hp * LANES, (hp + 1) * LANES) for hp in range(n_pairs)]
    streams = [(b, hp) for b in range(nbat) for hp in range(n_pairs)]
    ids = range(len(streams))

    @pl.when(pl.program_id(1) == 0)
    def _():
        s_ref[...] = jnp.zeros_like(s_ref)

    lane = lax.broadcasted_iota(jnp.int32, (1, LANES), 1)
    lane_lo = lane < RWKV_HEAD
    ci = lax.broadcasted_iota(jnp.int32, (C, C), 0)
    cj = lax.broadcasted_iota(jnp.int32, (C, C), 1)
    tri = jnp.where(ci >= cj, 1.0, 0.0).astype(BF16)
    pi = lax.broadcasted_iota(jnp.int32, (P, P), 0)
    pj = lax.broadcasted_iota(jnp.int32, (P, P), 1)
    same = (pi // C) == (pj // C)
    strict = jnp.logical_and(same, (pi % C) > (pj % C))
    incl = jnp.logical_and(same, (pi % C) >= (pj % C))
    eye = jnp.where(pi == pj, 1.0, 0.0).astype(F32)

    def stack(x):
        return jnp.concatenate([jnp.where(lane_lo, x, 0.0), jnp.where(lane_lo, 0.0, x)], axis=0)

    def chunk(c, carry):
        rows = pl.ds(pl.multiple_of(c * C, C), C)
        at, rt, kh, bh, vv, d_end = [], [], [], [], [], []
        for b in range(nbat):
            lw = lw_ref[b, rows, :]
            cum = _cumsum_rows(tri, lw)
            e_pos = jnp.exp(cum)
            e_neg = jnp.exp(-cum)
            d_end.append(e_pos[C - 1:C, :])
            rt.append(r_ref[b, rows, :] * e_pos)
            at.append(na_ref[b, rows, :] * jnp.exp(cum - lw))
            kh.append(k_ref[b, rows, :] * e_neg)
            bh.append(bb_ref[b, rows, :] * e_neg)
            vv.append(v_ref[b, rows, :])
        x2 = [jnp.concatenate([stack(at[b][:, sls[hp]]), stack(rt[b][:, sls[hp]])], axis=0).astype(BF16)
              for b, hp in streams]
        y2 = [jnp.concatenate([stack(kh[b][:, sls[hp]]), stack(bh[b][:, sls[hp]])], axis=0).astype(BF16)
              for b, hp in streams]
        vs = [stack(vv[b][:, sls[hp]]).astype(BF16) for b, hp in streams]
        gram = [_dot_nt(x2[i], y2[i]) for i in ids]
        s = [s_ref[i] for i in ids]
        xs = [_dot_nt(x2[i], s[i].astype(BF16)) for i in ids]
        a_ab = [jnp.where(strict, gram[i][:P, P:], 0.0) for i in ids]
        a_kr = [jnp.concatenate([jnp.where(strict, gram[i][:P, :P], 0.0),
                                 jnp.where(incl, gram[i][P:, :P], 0.0)], axis=0).astype(BF16) for i in ids]
        a_rb = [jnp.where(incl, gram[i][P:, P:], 0.0).astype(BF16) for i in ids]
        av = [_dot_nn(a_kr[i], vs[i]) for i in ids]
        inv = [eye + a_ab[i] for i in ids]
        q = [a_ab[i].astype(BF16) for i in ids]
        q = [_dot_nn(q[i], q[i]).astype(BF16) for i in ids]
        for _ in range(int(math.log2(C)) - 2):
            t = [_dot_nn(jnp.concatenate([q[i], inv[i].astype(BF16)], axis=0), q[i]) for i in ids]
            inv = [inv[i] + t[i][P:] for i in ids]
            q = [t[i][:P].astype(BF16) for i in ids]
        inv = [inv[i] + _dot_nn(inv[i].astype(BF16), q[i]) for i in ids]
        u = [_dot_nn(inv[i].astype(BF16), (xs[i][:P] + av[i][:P]).astype(BF16)).astype(BF16) for i in ids]
        yu = [_dot_nn(a_rb[i], u[i]) for i in ids]
        upd = [_dot_tn(jnp.concatenate([vs[i], u[i]], axis=0), y2[i]) for i in ids]
        for i, (b, hp) in enumerate(streams):
            ys = xs[i][P:] + av[i][P:] + yu[i]
            yacc_ref[b, rows, sls[hp]] = ys[:C] + ys[C:]
            s_ref[i] = s[i] * d_end[b][:, sls[hp]] + upd[i] * d_end[b][:, sls[hp]]
        return carry

    lax.fori_loop(0, sb // C, chunk, 0)

    li = lax.broadcasted_iota(jnp.int32, (LANES, LANES), 0)
    lj = lax.broadcasted_iota(jnp.int32, (LANES, LANES), 1)
    head_ones = jnp.where((li // RWKV_HEAD) == (lj // RWKV_HEAD), 1.0, 0.0).astype(BF16)

    def head_sums2(x):
        hi = x.astype(BF16)
        lo = (x - hi.astype(F32)).astype(BF16)
        return _dot_nn(hi, head_ones) + _dot_nn(lo, head_ones)

    mean = [head_sums2(yacc_ref[b, :, sls[hp]]) * (1.0 / RWKV_HEAD) for b, hp in streams]
    rks = [_dot_nn((r_ref[b, :, sls[hp]] * k_ref[b, :, sls[hp]] * rk_ref[:, sls[hp]]).astype(BF16), head_ones)
           for b, hp in streams]
    d = [yacc_ref[b, :, sls[hp]] - mean[i] for i, (b, hp) in enumerate(streams)]
    var = [head_sums2(d[i] * d[i]) * (1.0 / RWKV_HEAD) for i in ids]
    for i, (b, hp) in enumerate(streams):
        sl = sls[hp]
        yn = d[i] * lax.rsqrt(var[i] + GN_EPS) * gnw_ref[:, sl] + gnb_ref[:, sl]
        y_o[b, :, sl] = ((yn + rks[i] * v_ref[b, :, sl]) * g_ref[b, :, sl]).astype(BF16)


def _scan(r, k, v, na, bb, lw, g, gn_w, gn_b, r_k, *, batch, seq, sb, nbat):
    T, W = r.shape
    n_streams = nbat * (W // LANES)
    blk = lambda b, j: (b, j, 0)
    const = lambda b, j: (0, 0)
    in_specs = [pl.BlockSpec((nbat, sb, W), blk)] * 7 + [pl.BlockSpec((1, W), const)] * 3
    seq3 = lambda a: a.reshape(batch, seq, W)
    y = pl.pallas_call(
        functools.partial(_scan_kernel, sb=sb, nbat=nbat),
        grid=(batch // nbat, seq // sb),
        in_specs=in_specs,
        out_specs=pl.BlockSpec((nbat, sb, W), blk),
        out_shape=jax.ShapeDtypeStruct((batch, seq, W), BF16),
        scratch_shapes=[pltpu.VMEM((n_streams, LANES, LANES), F32), pltpu.VMEM((nbat, sb, W), F32)],
        compiler_params=pltpu.CompilerParams(
            dimension_semantics=("arbitrary", "arbitrary"), vmem_limit_bytes=VMEM_LIMIT),
        name="rwkv_scan",
    )(*(seq3(a) for a in (r, k, v, na, bb, lw, g)), gn_w, gn_b, r_k)
    return y.reshape(T, W)


def _tile(n, pref):
    return pref if n % pref == 0 else CHUNK


def kernel(x, ffn1_norm, ffn1_w1, ffn1_w3, ffn1_w2, mix_norm, w_in, mu_shift, w0, w_lora_up, a0, a_lora_up, g_lora_up, k_k, k_a, r_k, gn_w, gn_b, sgu_ln_g, sgu_ln_b, sgu_w, sgu_b, w_out, ffn2_norm, ffn2_w1, ffn2_w3, ffn2_w2, final_norm):
    B, S, D = x.shape
    T = B * S
    depth = ffn1_norm.shape[0]
    d_ff = ffn1_w1.shape[-1]
    tf = d_ff // 2 if (d_ff // 2) % LANES == 0 else d_ff
    tm_ffn = _tile(S, 512)
    tm_mix = _tile(S, 256)
    sb = _tile(S, 256)
    rowv = lambda a: a.reshape(1, -1).astype(F32)

    xt = x.reshape(T, D)
    for l in range(depth):
        xt = _ffn(xt, rowv(ffn1_norm[l]), ffn1_w1[l].astype(BF16), ffn1_w3[l].astype(BF16),
                  ffn1_w2[l].astype(BF16), tm=tm_ffn, tf=tf)
        zeros = jnp.zeros((DECAY_LORA, RWKV_WIDTH), F32)
        lora = jnp.concatenate([jnp.concatenate([w_lora_up[l], zeros], axis=1),
                                jnp.concatenate([zeros, a_lora_up[l]], axis=1)], axis=0)
        sgu_bias = jnp.repeat(jnp.transpose(sgu_b[l]), GMLP_WIDTH // GMLP_GROUPS, axis=1)
        r, k, v, na, bb, lw, g, y_gmlp = _mixin(
            xt, rowv(mix_norm[l]), w_in[l].astype(BF16), rowv(mu_shift[l]), rowv(w0[l]), rowv(a0[l]),
            lora.astype(BF16), g_lora_up[l].astype(BF16), rowv(k_k[l]), rowv(k_a[l]),
            rowv(sgu_ln_g[l]), rowv(sgu_ln_b[l]), sgu_w[l], sgu_bias, tm=tm_mix, seq=S)
        y_rwkv = _scan(r, k, v, na, bb, lw, g, rowv(gn_w[l]), rowv(gn_b[l]), rowv(r_k[l]),
                       batch=B, seq=S, sb=sb, nbat=SCAN_BATCH if B % SCAN_BATCH == 0 else 1)
        last = l == depth - 1
        xt = _ffn(xt, rowv(ffn2_norm[l]), ffn2_w1[l].astype(BF16), ffn2_w3[l].astype(BF16),
                  ffn2_w2[l].astype(BF16), tm=tm_ffn, tf=tf,
                  mix_in=(y_rwkv, y_gmlp, w_out[l].astype(BF16)),
                  final_g=rowv(final_norm) if last else None)
    if depth == 0:
        raise ValueError("depth must be positive")
    return xt.reshape(B, S, D)
```

```python
import functools
import math

import jax
import jax.numpy as jnp
from jax import lax
from jax.experimental import pallas as pl
from jax.experimental.pallas import tpu as pltpu

F32 = jnp.float32
BF16 = jnp.bfloat16

RWKV_WIDTH = 512
RWKV_HEAD = 64
GMLP_WIDTH = 512
GMLP_GROUPS = 8
CHUNK = 128
DECAY_LORA = 64
AAA_LORA = 64
GATE_LORA = 128
D_SHIFT = 3 * RWKV_WIDTH + DECAY_LORA + AAA_LORA + GATE_LORA
RMS_EPS = 1e-6
GN_EPS = 64e-5
LN_EPS = 1e-5
FFN_RES_SCALE = 0.5

LANES = 128
SCAN_CHUNK = 64
SCAN_BATCH = 4
VMEM_LIMIT = 56 * 1024 * 1024


def _rmsnorm(x, g):
    return x * lax.rsqrt(jnp.mean(x * x, axis=-1, keepdims=True) + RMS_EPS) * g


def _gelu(x):
    return 0.5 * x * (1.0 + lax.erf(x * (1.0 / math.sqrt(2.0))))


def _sigmoid(x):
    return 1.0 / (1.0 + jnp.exp(-x))


def _softplus(x):
    return jnp.maximum(x, 0.0) + jnp.log1p(jnp.exp(-jnp.abs(x)))


def _bdot(a, b):
    return jnp.dot(a.astype(BF16), b.astype(BF16), preferred_element_type=F32)


def _ffn_kernel(*refs, mix, final):
    it = iter(refs)
    x_ref = next(it)
    if mix:
        yr_ref, yg_ref, wo_ref = next(it), next(it), next(it)
    g_ref, w1_ref, w3_ref, w2_ref = next(it), next(it), next(it), next(it)
    fg_ref = next(it) if final else None
    o_ref = next(it)

    x = x_ref[...]
    if mix:
        x = x + jnp.dot(yr_ref[...], wo_ref[:RWKV_WIDTH, :], preferred_element_type=F32)
        x = x + jnp.dot(yg_ref[...], wo_ref[RWKV_WIDTH:, :], preferred_element_type=F32)
    h = _rmsnorm(x, g_ref[...]).astype(BF16)
    a = jnp.dot(h, w1_ref[...], preferred_element_type=F32)
    b = jnp.dot(h, w3_ref[...], preferred_element_type=F32)
    z = (a * _sigmoid(a) * b).astype(BF16)
    y = x + FFN_RES_SCALE * jnp.dot(z, w2_ref[...], preferred_element_type=F32)
    if final:
        y = _rmsnorm(y, fg_ref[...])
    o_ref[...] = y


def _ffn(x, norm_g, w1, w3, w2, *, tm, mix_in=None, final_g=None):
    T, D = x.shape
    mix = mix_in is not None
    final = final_g is not None
    row = lambda i: (i, 0)
    whole = lambda a: pl.BlockSpec(a.shape, lambda i: (0, 0), pipeline_mode=pl.Buffered(1))
    in_specs = [pl.BlockSpec((tm, D), row)]
    args = [x]
    if mix:
        yr, yg, wo = mix_in
        in_specs += [pl.BlockSpec((tm, yr.shape[1]), row), pl.BlockSpec((tm, yg.shape[1]), row), whole(wo)]
        args += [yr, yg, wo]
    in_specs += [whole(norm_g), whole(w1), whole(w3), whole(w2)]
    args += [norm_g, w1, w3, w2]
    if final:
        in_specs.append(whole(final_g))
        args.append(final_g)
    return pl.pallas_call(
        functools.partial(_ffn_kernel, mix=mix, final=final),
        grid=(T // tm,),
        in_specs=in_specs,
        out_specs=pl.BlockSpec((tm, D), row),
        out_shape=jax.ShapeDtypeStruct((T, D), F32),
        compiler_params=pltpu.CompilerParams(
            dimension_semantics=("arbitrary",), vmem_limit_bytes=VMEM_LIMIT),
        name="ffn_mix_final" if mix else "ffn",
    )(*args)


def _head_sums(x, lane_lo):
    s0 = jnp.sum(jnp.where(lane_lo, x, 0.0), axis=-1, keepdims=True)
    s1 = jnp.sum(jnp.where(lane_lo, 0.0, x), axis=-1, keepdims=True)
    return jnp.where(lane_lo, s0, s1)


def _mixin_kernel(x_ref, g_ref, win_ref, mu_ref, w0_ref, a0_ref, lora_ref, gup_ref, kk_ref, ka_ref,
                  lng_ref, lnb_ref, sw_ref, sb_ref,
                  r_o, k_o, v_o, na_o, bb_o, lw_o, g_o, ygm_o, prev_ref, *, tm, tiles_per_seq):
    i = pl.program_id(0)
    W = RWKV_WIDTH
    h = _rmsnorm(x_ref[...], g_ref[...]).astype(BF16)
    p = jnp.dot(h, win_ref[...], preferred_element_type=F32)

    ps = p[:, :D_SHIFT]
    rolled = pltpu.roll(ps, 1, 0)
    first = (i % tiles_per_seq) == 0
    prev = jnp.where(first, 0.0, prev_ref[...])
    row = lax.broadcasted_iota(jnp.int32, (tm, 1), 0)
    shifted = jnp.where(row == 0, prev, rolled)
    prev_ref[...] = ps[tm - 1:tm, :]
    ps = ps + (shifted - ps) * mu_ref[...]

    r = ps[:, :W]
    k = ps[:, W:2 * W]
    v = ps[:, 2 * W:3 * W]
    slab = ps[:, 3 * W:3 * W + DECAY_LORA + AAA_LORA]
    pg = ps[:, 3 * W + DECAY_LORA + AAA_LORA:]
    lane = lax.broadcasted_iota(jnp.int32, (1, LANES), 1)
    slab = jnp.where(lane < DECAY_LORA, jnp.tanh(slab), slab)
    lora = _bdot(slab, lora_ref[...])
    w = -_softplus(-(w0_ref[...] + lora[:, :W])) - 0.5
    lw_o[...] = -jnp.exp(w)
    a = _sigmoid(a0_ref[...] + lora[:, W:])
    g_o[...] = _bdot(_sigmoid(pg), gup_ref[...])

    lane_lo = lane < RWKV_HEAD
    kk = k * kk_ref[...]
    for hp in range(W // LANES):
        sl = slice(hp * LANES, (hp + 1) * LANES)
        kkp = kk[:, sl]
        ss = _head_sums(kkp * kkp, lane_lo)
        kkn = kkp / jnp.maximum(jnp.sqrt(ss), 1e-12)
        na_o[:, sl] = -kkn
        bb_o[:, sl] = kkn * a[:, sl]
    r_o[...] = r
    k_o[...] = k * (1.0 + (a - 1.0) * ka_ref[...])
    v_o[...] = v

    u = _gelu(p[:, D_SHIFT:D_SHIFT + GMLP_WIDTH])
    vg = _gelu(p[:, D_SHIFT + GMLP_WIDTH:])
    mean = jnp.mean(vg, axis=-1, keepdims=True)
    d = vg - mean
    var = jnp.mean(d * d, axis=-1, keepdims=True)
    vn = (d * lax.rsqrt(var + LN_EPS) * lng_ref[...] + lnb_ref[...]).astype(BF16)
    tr = lax.broadcasted_iota(jnp.int32, (CHUNK, CHUNK), 0)
    tc = lax.broadcasted_iota(jnp.int32, (CHUNK, CHUNK), 1)
    ws = [jnp.where(tr >= tc, sw_ref[gi], 0.0).astype(BF16) for gi in range(GMLP_GROUPS)]
    bias = sb_ref[...]
    for c in range(tm // CHUNK):
        rows = slice(c * CHUNK, (c + 1) * CHUNK)
        outs = []
        for hp in range(GMLP_WIDTH // LANES):
            vp = vn[rows, hp * LANES:(hp + 1) * LANES]
            m0 = jnp.dot(ws[2 * hp], vp, preferred_element_type=F32)
            m1 = jnp.dot(ws[2 * hp + 1], vp, preferred_element_type=F32)
            outs.append(jnp.where(lane_lo, m0, m1))
        mixed = jnp.concatenate(outs, axis=1) + bias
        ygm_o[rows, :] = (u[rows, :] * mixed).astype(BF16)


def _mixin(x1, mix_g, w_in, mu, w0, a0, lora, g_up, k_k, k_a, ln_g, ln_b, sgu_w, sgu_bias, *, tm, seq):
    T, D = x1.shape
    W = RWKV_WIDTH
    row = lambda i: (i, 0)
    const = lambda i: (0, 0)
    full = lambda a: pl.BlockSpec(a.shape, (lambda i: (0,) * a.ndim))
    in_specs = [pl.BlockSpec((tm, D), row)] + [full(a) for a in
                (mix_g, w_in, mu, w0, a0, lora, g_up, k_k, k_a, ln_g, ln_b, sgu_w, sgu_bias)]
    out_shape = [jax.ShapeDtypeStruct((T, W), F32)] * 7 + [jax.ShapeDtypeStruct((T, GMLP_WIDTH), BF16)]
    out_specs = [pl.BlockSpec((tm, W), row)] * 7 + [pl.BlockSpec((tm, GMLP_WIDTH), row)]
    return pl.pallas_call(
        functools.partial(_mixin_kernel, tm=tm, tiles_per_seq=seq // tm),
        grid=(T // tm,),
        in_specs=in_specs,
        out_specs=out_specs,
        out_shape=out_shape,
        scratch_shapes=[pltpu.VMEM((1, D_SHIFT), F32)],
        compiler_params=pltpu.CompilerParams(
            dimension_semantics=("arbitrary",), vmem_limit_bytes=VMEM_LIMIT),
        name="mixin",
    )(x1, mix_g, w_in, mu, w0, a0, lora, g_up, k_k, k_a, ln_g, ln_b, sgu_w, sgu_bias)


def _dot_nn(a, b):
    return jnp.dot(a, b, preferred_element_type=F32)


def _dot_nt(a, b):
    return lax.dot_general(a, b, (((1,), (1,)), ((), ())), preferred_element_type=F32)


def _dot_tn(a, b):
    return lax.dot_general(a, b, (((0,), (0,)), ((), ())), preferred_element_type=F32)


def _cumsum_rows(tri, x):
    out = None
    for _ in range(3):
        hi = x.astype(BF16)
        t = _dot_nn(tri, hi)
        out = t if out is None else out + t
        x = x - hi.astype(F32)
    return out


def _scan_kernel(r_ref, k_ref, v_ref, na_ref, bb_ref, lw_ref, g_ref, gnw_ref, gnb_ref, rk_ref,
                 y_o, s_ref, yacc_ref, *, sb, nbat):
    C = SCAN_CHUNK
    P = 2 * C
    n_pairs = RWKV_WIDTH // LANES
    sls = [slice(hp * LANES, (hp + 1) * LANES) for hp in range(n_pairs)]
    streams = [(b, hp) for b in range(nbat) for hp in range(n_pairs)]
    ids = range(len(streams))

    @pl.when(pl.program_id(1) == 0)
    def _():
        s_ref[...] = jnp.zeros_like(s_ref)

    lane = lax.broadcasted_iota(jnp.int32, (1, LANES), 1)
    lane_lo = lane < RWKV_HEAD
    ci = lax.broadcasted_iota(jnp.int32, (C, C), 0)
    cj = lax.broadcasted_iota(jnp.int32, (C, C), 1)
    tri = jnp.where(ci >= cj, 1.0, 0.0).astype(BF16)
    pi = lax.broadcasted_iota(jnp.int32, (P, P), 0)
    pj = lax.broadcasted_iota(jnp.int32, (P, P), 1)
    same = (pi // C) == (pj // C)
    strict = jnp.logical_and(same, (pi % C) > (pj % C))
    incl = jnp.logical_and(same, (pi % C) >= (pj % C))
    eye = jnp.where(pi == pj, 1.0, 0.0).astype(F32)

    def stack(x):
        return jnp.concatenate([jnp.where(lane_lo, x, 0.0), jnp.where(lane_lo, 0.0, x)], axis=0)

    def chunk(c, carry):
        rows = pl.ds(pl.multiple_of(c * C, C), C)
        at, rt, kh, bh, vv, d_end = [], [], [], [], [], []
        for b in range(nbat):
            lw = lw_ref[b, rows, :]
            cum = _cumsum_rows(tri, lw)
            e_pos = jnp.exp(cum)
            e_neg = jnp.exp(-cum)
            d_end.append(e_pos[C - 1:C, :])
            rt.append(r_ref[b, rows, :] * e_pos)
            at.append(na_ref[b, rows, :] * jnp.exp(cum - lw))
            kh.append(k_ref[b, rows, :] * e_neg)
            bh.append(bb_ref[b, rows, :] * e_neg)
            vv.append(v_ref[b, rows, :])
        x2 = [jnp.concatenate([stack(at[b][:, sls[hp]]), stack(rt[b][:, sls[hp]])], axis=0).astype(BF16)
              for b, hp in streams]
        y2 = [jnp.concatenate([stack(kh[b][:, sls[hp]]), stack(bh[b][:, sls[hp]])], axis=0).astype(BF16)
              for b, hp in streams]
        vs = [stack(vv[b][:, sls[hp]]).astype(BF16) for b, hp in streams]
        gram = [_dot_nt(x2[i], y2[i]) for i in ids]
        s = [s_ref[i] for i in ids]
        xs = [_dot_nt(x2[i], s[i].astype(BF16)) for i in ids]
        a_ab = [jnp.where(strict, gram[i][:P, P:], 0.0) for i in ids]
        a_kr = [jnp.concatenate([jnp.where(strict, gram[i][:P, :P], 0.0),
                                 jnp.where(incl, gram[i][P:, :P], 0.0)], axis=0).astype(BF16) for i in ids]
        a_rb = [jnp.where(incl, gram[i][P:, P:], 0.0).astype(BF16) for i in ids]
        av = [_dot_nn(a_kr[i], vs[i]) for i in ids]
        inv = [eye + a_ab[i] for i in ids]
        q = [a_ab[i].astype(BF16) for i in ids]
        q = [_dot_nn(q[i], q[i]).astype(BF16) for i in ids]
        for _ in range(int(math.log2(C)) - 2):
            t = [_dot_nn(jnp.concatenate([q[i], inv[i].astype(BF16)], axis=0), q[i]) for i in ids]
            inv = [inv[i] + t[i][P:] for i in ids]
            q = [t[i][:P].astype(BF16) for i in ids]
        inv = [inv[i] + _dot_nn(inv[i].astype(BF16), q[i]) for i in ids]
        u = [_dot_nn(inv[i].astype(BF16), (xs[i][:P] + av[i][:P]).astype(BF16)).astype(BF16) for i in ids]
        yu = [_dot_nn(a_rb[i], u[i]) for i in ids]
        upd = [_dot_tn(jnp.concatenate([vs[i], u[i]], axis=0), y2[i]) for i in ids]
        for i, (b, hp) in enumerate(streams):
            ys = xs[i][P:] + av[i][P:] + yu[i]
            yacc_ref[b, rows, sls[hp]] = ys[:C] + ys[C:]
            s_ref[i] = s[i] * d_end[b][:, sls[hp]] + upd[i] * d_end[b][:, sls[hp]]
        return carry

    lax.fori_loop(0, sb // C, chunk, 0)

    li = lax.broadcasted_iota(jnp.int32, (LANES, LANES), 0)
    lj = lax.broadcasted_iota(jnp.int32, (LANES, LANES), 1)
    head_ones = jnp.where((li // RWKV_HEAD) == (lj // RWKV_HEAD), 1.0, 0.0).astype(BF16)

    def head_sums2(x):
        hi = x.astype(BF16)
        lo = (x - hi.astype(F32)).astype(BF16)
        return _dot_nn(hi, head_ones) + _dot_nn(lo, head_ones)

    mean = [head_sums2(yacc_ref[b, :, sls[hp]]) * (1.0 / RWKV_HEAD) for b, hp in streams]
    rks = [_dot_nn((r_ref[b, :, sls[hp]] * k_ref[b, :, sls[hp]] * rk_ref[:, sls[hp]]).astype(BF16), head_ones)
           for b, hp in streams]
    d = [yacc_ref[b, :, sls[hp]] - mean[i] for i, (b, hp) in enumerate(streams)]
    var = [head_sums2(d[i] * d[i]) * (1.0 / RWKV_HEAD) for i in ids]
    for i, (b, hp) in enumerate(streams):
        sl = sls[hp]
        yn = d[i] * lax.rsqrt(var[i] + GN_EPS) * gnw_ref[:, sl] + gnb_ref[:, sl]
        y_o[b, :, sl] = ((yn + rks[i] * v_ref[b, :, sl]) * g_ref[b, :, sl]).astype(BF16)


def _scan(r, k, v, na, bb, lw, g, gn_w, gn_b, r_k, *, batch, seq, sb, nbat):
    T, W = r.shape
    n_streams = nbat * (W // LANES)
    blk = lambda b, j: (b, j, 0)
    const = lambda b, j: (0, 0)
    in_specs = [pl.BlockSpec((nbat, sb, W), blk)] * 7 + [pl.BlockSpec((1, W), const)] * 3
    seq3 = lambda a: a.reshape(batch, seq, W)
    y = pl.pallas_call(
        functools.partial(_scan_kernel, sb=sb, nbat=nbat),
        grid=(batch // nbat, seq // sb),
        in_specs=in_specs,
        out_specs=pl.BlockSpec((nbat, sb, W), blk),
        out_shape=jax.ShapeDtypeStruct((batch, seq, W), BF16),
        scratch_shapes=[pltpu.VMEM((n_streams, LANES, LANES), F32), pltpu.VMEM((nbat, sb, W), F32)],
        compiler_params=pltpu.CompilerParams(
            dimension_semantics=("arbitrary", "arbitrary"), vmem_limit_bytes=VMEM_LIMIT),
        name="rwkv_scan",
    )(*(seq3(a) for a in (r, k, v, na, bb, lw, g)), gn_w, gn_b, r_k)
    return y.reshape(T, W)


def _tile(n, pref):
    return pref if n % pref == 0 else CHUNK


def kernel(x, ffn1_norm, ffn1_w1, ffn1_w3, ffn1_w2, mix_norm, w_in, mu_shift, w0, w_lora_up, a0, a_lora_up, g_lora_up, k_k, k_a, r_k, gn_w, gn_b, sgu_ln_g, sgu_ln_b, sgu_w, sgu_b, w_out, ffn2_norm, ffn2_w1, ffn2_w3, ffn2_w2, final_norm):
    B, S, D = x.shape
    T = B * S
    depth = ffn1_norm.shape[0]
    tm_ffn = _tile(S, 512)
    tm_mix = _tile(S, 256)
    sb = _tile(S, 256)
    rowv = lambda a: a.reshape(1, -1).astype(F32)

    xt = x.reshape(T, D)
    for l in range(depth):
        xt = _ffn(xt, rowv(ffn1_norm[l]), ffn1_w1[l].astype(BF16), ffn1_w3[l].astype(BF16),
                  ffn1_w2[l].astype(BF16), tm=tm_ffn)
        zeros = jnp.zeros((DECAY_LORA, RWKV_WIDTH), F32)
        lora = jnp.concatenate([jnp.concatenate([w_lora_up[l], zeros], axis=1),
                                jnp.concatenate([zeros, a_lora_up[l]], axis=1)], axis=0)
        sgu_bias = jnp.repeat(jnp.transpose(sgu_b[l]), GMLP_WIDTH // GMLP_GROUPS, axis=1)
        r, k, v, na, bb, lw, g, y_gmlp = _mixin(
            xt, rowv(mix_norm[l]), w_in[l].astype(BF16), rowv(mu_shift[l]), rowv(w0[l]), rowv(a0[l]),
            lora.astype(BF16), g_lora_up[l].astype(BF16), rowv(k_k[l]), rowv(k_a[l]),
            rowv(sgu_ln_g[l]), rowv(sgu_ln_b[l]), sgu_w[l], sgu_bias, tm=tm_mix, seq=S)
        y_rwkv = _scan(r, k, v, na, bb, lw, g, rowv(gn_w[l]), rowv(gn_b[l]), rowv(r_k[l]),
                       batch=B, seq=S, sb=sb, nbat=SCAN_BATCH if B % SCAN_BATCH == 0 else 1)
        last = l == depth - 1
        xt = _ffn(xt, rowv(ffn2_norm[l]), ffn2_w1[l].astype(BF16), ffn2_w3[l].astype(BF16),
                  ffn2_w2[l].astype(BF16), tm=tm_ffn,
                  mix_in=(y_rwkv, y_gmlp, w_out[l].astype(BF16)),
                  final_g=rowv(final_norm) if last else None)
    if depth == 0:
        raise ValueError("depth must be positive")
    return xt.reshape(B, S, D)
```

```python
import functools
import math

import jax
import jax.numpy as jnp
from jax import lax
from jax.experimental import pallas as pl
from jax.experimental.pallas import tpu as pltpu

F32 = jnp.float32
BF16 = jnp.bfloat16

RWKV_WIDTH = 512
RWKV_HEAD = 64
GMLP_WIDTH = 512
GMLP_GROUPS = 8
CHUNK = 128
DECAY_LORA = 64
AAA_LORA = 64
GATE_LORA = 128
D_SHIFT = 3 * RWKV_WIDTH + DECAY_LORA + AAA_LORA + GATE_LORA
RMS_EPS = 1e-6
GN_EPS = 64e-5
LN_EPS = 1e-5
FFN_RES_SCALE = 0.5

LANES = 128
SCAN_CHUNK = 64
SCAN_BATCH = 4
VMEM_LIMIT = 56 * 1024 * 1024


def _rmsnorm(x, g):
    return x * lax.rsqrt(jnp.mean(x * x, axis=-1, keepdims=True) + RMS_EPS) * g


def _gelu(x):
    return 0.5 * x * (1.0 + lax.erf(x * (1.0 / math.sqrt(2.0))))


def _sigmoid(x):
    return 1.0 / (1.0 + jnp.exp(-x))


def _bdot(a, b):
    return jnp.dot(a.astype(BF16), b.astype(BF16), preferred_element_type=F32)


def _ffn_kernel(*refs, mix, final):
    it = iter(refs)
    x_ref = next(it)
    if mix:
        yr_ref, yg_ref, wo_ref = next(it), next(it), next(it)
    g_ref, w1_ref, w3_ref, w2_ref = next(it), next(it), next(it), next(it)
    fg_ref = next(it) if final else None
    o_ref = next(it)

    x = x_ref[...]
    if mix:
        x = x + jnp.dot(yr_ref[...], wo_ref[:RWKV_WIDTH, :], preferred_element_type=F32)
        x = x + jnp.dot(yg_ref[...], wo_ref[RWKV_WIDTH:, :], preferred_element_type=F32)
    h = _rmsnorm(x, g_ref[...]).astype(BF16)
    a = jnp.dot(h, w1_ref[...], preferred_element_type=F32)
    b = jnp.dot(h, w3_ref[...], preferred_element_type=F32)
    z = (a * _sigmoid(a) * b).astype(BF16)
    y = x + FFN_RES_SCALE * jnp.dot(z, w2_ref[...], preferred_element_type=F32)
    if final:
        y = _rmsnorm(y, fg_ref[...])
    o_ref[...] = y


def _ffn(x, norm_g, w1, w3, w2, *, tm, mix_in=None, final_g=None):
    T, D = x.shape
    mix = mix_in is not None
    final = final_g is not None
    row = lambda i: (i, 0)
    whole = lambda a: pl.BlockSpec(a.shape, lambda i: (0, 0), pipeline_mode=pl.Buffered(1))
    in_specs = [pl.BlockSpec((tm, D), row)]
    args = [x]
    if mix:
        yr, yg, wo = mix_in
        in_specs += [pl.BlockSpec((tm, yr.shape[1]), row), pl.BlockSpec((tm, yg.shape[1]), row), whole(wo)]
        args += [yr, yg, wo]
    in_specs += [whole(norm_g), whole(w1), whole(w3), whole(w2)]
    args += [norm_g, w1, w3, w2]
    if final:
        in_specs.append(whole(final_g))
        args.append(final_g)
    return pl.pallas_call(
        functools.partial(_ffn_kernel, mix=mix, final=final),
        grid=(T // tm,),
        in_specs=in_specs,
        out_specs=pl.BlockSpec((tm, D), row),
        out_shape=jax.ShapeDtypeStruct((T, D), F32),
        compiler_params=pltpu.CompilerParams(
            dimension_semantics=("arbitrary",), vmem_limit_bytes=VMEM_LIMIT),
        name="ffn_mix_final" if mix else "ffn",
    )(*args)


def _head_sums(x, lane_lo):
    s0 = jnp.sum(jnp.where(lane_lo, x, 0.0), axis=-1, keepdims=True)
    s1 = jnp.sum(jnp.where(lane_lo, 0.0, x), axis=-1, keepdims=True)
    return jnp.where(lane_lo, s0, s1)


def _mixin_kernel(x_ref, g_ref, win_ref, mu_ref, w0_ref, a0_ref, lora_ref, gup_ref, kk_ref, ka_ref,
                  lng_ref, lnb_ref, sw_ref, sb_ref,
                  r_o, k_o, v_o, na_o, bb_o, lw_o, g_o, ygm_o, prev_ref, *, tm, tiles_per_seq):
    i = pl.program_id(0)
    W = RWKV_WIDTH
    h = _rmsnorm(x_ref[...], g_ref[...]).astype(BF16)
    p = jnp.dot(h, win_ref[...], preferred_element_type=F32)

    ps = p[:, :D_SHIFT]
    rolled = pltpu.roll(ps, 1, 0)
    first = (i % tiles_per_seq) == 0
    prev = jnp.where(first, 0.0, prev_ref[...])
    sub = 8
    row = lax.broadcasted_iota(jnp.int32, (sub, 1), 0)
    shifted = jnp.concatenate([jnp.where(row == 0, prev, rolled[:sub]), rolled[sub:]], axis=0)
    prev_ref[...] = ps[tm - 1:tm, :]
    ps = ps + (shifted - ps) * mu_ref[...]

    r = ps[:, :W]
    k = ps[:, W:2 * W]
    v = ps[:, 2 * W:3 * W]
    slab = ps[:, 3 * W:3 * W + DECAY_LORA + AAA_LORA]
    pg = ps[:, 3 * W + DECAY_LORA + AAA_LORA:]
    lane = lax.broadcasted_iota(jnp.int32, (1, LANES), 1)
    slab = jnp.where(lane < DECAY_LORA, jnp.tanh(slab), slab)
    lora = _bdot(slab, lora_ref[...])
    lw_o[...] = -math.exp(-0.5) * _sigmoid(w0_ref[...] + lora[:, :W])
    a = _sigmoid(a0_ref[...] + lora[:, W:])
    g_o[...] = _bdot(_sigmoid(pg), gup_ref[...])

    lane_lo = lane < RWKV_HEAD
    kk = k * kk_ref[...]
    for hp in range(W // LANES):
        sl = slice(hp * LANES, (hp + 1) * LANES)
        kkp = kk[:, sl]
        ss = _head_sums(kkp * kkp, lane_lo)
        kkn = kkp * jnp.minimum(lax.rsqrt(ss), 1e12)
        na_o[:, sl] = -kkn
        bb_o[:, sl] = kkn * a[:, sl]
    r_o[...] = r
    k_o[...] = k * (1.0 + (a - 1.0) * ka_ref[...])
    v_o[...] = v

    u = _gelu(p[:, D_SHIFT:D_SHIFT + GMLP_WIDTH])
    vg = _gelu(p[:, D_SHIFT + GMLP_WIDTH:])
    mean = jnp.mean(vg, axis=-1, keepdims=True)
    d = vg - mean
    var = jnp.mean(d * d, axis=-1, keepdims=True)
    vn = (d * lax.rsqrt(var + LN_EPS) * lng_ref[...] + lnb_ref[...]).astype(BF16)
    tr = lax.broadcasted_iota(jnp.int32, (CHUNK, CHUNK), 0)
    tc = lax.broadcasted_iota(jnp.int32, (CHUNK, CHUNK), 1)
    ws = [jnp.where(tr >= tc, sw_ref[gi], 0.0).astype(BF16) for gi in range(GMLP_GROUPS)]
    bias = sb_ref[...]
    for c in range(tm // CHUNK):
        rows = slice(c * CHUNK, (c + 1) * CHUNK)
        outs = []
        for hp in range(GMLP_WIDTH // LANES):
            vp = vn[rows, hp * LANES:(hp + 1) * LANES]
            m0 = jnp.dot(ws[2 * hp], vp, preferred_element_type=F32)
            m1 = jnp.dot(ws[2 * hp + 1], vp, preferred_element_type=F32)
            outs.append(jnp.where(lane_lo, m0, m1))
        mixed = jnp.concatenate(outs, axis=1) + bias
        ygm_o[rows, :] = (u[rows, :] * mixed).astype(BF16)


def _mixin(x1, mix_g, w_in, mu, w0, a0, lora, g_up, k_k, k_a, ln_g, ln_b, sgu_w, sgu_bias, *, tm, seq):
    T, D = x1.shape
    W = RWKV_WIDTH
    row = lambda i: (i, 0)
    const = lambda i: (0, 0)
    full = lambda a: pl.BlockSpec(a.shape, (lambda i: (0,) * a.ndim))
    in_specs = [pl.BlockSpec((tm, D), row)] + [full(a) for a in
                (mix_g, w_in, mu, w0, a0, lora, g_up, k_k, k_a, ln_g, ln_b, sgu_w, sgu_bias)]
    out_shape = [jax.ShapeDtypeStruct((T, W), F32)] * 7 + [jax.ShapeDtypeStruct((T, GMLP_WIDTH), BF16)]
    out_specs = [pl.BlockSpec((tm, W), row)] * 7 + [pl.BlockSpec((tm, GMLP_WIDTH), row)]
    return pl.pallas_call(
        functools.partial(_mixin_kernel, tm=tm, tiles_per_seq=seq // tm),
        grid=(T // tm,),
        in_specs=in_specs,
        out_specs=out_specs,
        out_shape=out_shape,
        scratch_shapes=[pltpu.VMEM((1, D_SHIFT), F32)],
        compiler_params=pltpu.CompilerParams(
            dimension_semantics=("arbitrary",), vmem_limit_bytes=VMEM_LIMIT),
        name="mixin",
    )(x1, mix_g, w_in, mu, w0, a0, lora, g_up, k_k, k_a, ln_g, ln_b, sgu_w, sgu_bias)


def _dot_nn(a, b):
    return jnp.dot(a, b, preferred_element_type=F32)


def _dot_nt(a, b):
    return lax.dot_general(a, b, (((1,), (1,)), ((), ())), preferred_element_type=F32)


def _dot_tn(a, b):
    return lax.dot_general(a, b, (((0,), (0,)), ((), ())), preferred_element_type=F32)


def _cumsum_rows(tri, x):
    out = None
    for _ in range(3):
        hi = x.astype(BF16)
        t = _dot_nn(tri, hi)
        out = t if out is None else out + t
        x = x - hi.astype(F32)
    return out


def _scan_kernel(r_ref, k_ref, v_ref, na_ref, bb_ref, lw_ref, g_ref, gnw_ref, gnb_ref, rk_ref,
                 y_o, s_ref, yacc_ref, *, sb, nbat):
    C = SCAN_CHUNK
    P = 2 * C
    n_pairs = RWKV_WIDTH // LANES
    sls = [slice(hp * LANES, (hp + 1) * LANES) for hp in range(n_pairs)]
    streams = [(b, hp) for b in range(nbat) for hp in range(n_pairs)]
    ids = range(len(streams))

    @pl.when(pl.program_id(1) == 0)
    def _():
        s_ref[...] = jnp.zeros_like(s_ref)

    lane = lax.broadcasted_iota(jnp.int32, (1, LANES), 1)
    lane_lo = lane < RWKV_HEAD
    ci = lax.broadcasted_iota(jnp.int32, (C, C), 0)
    cj = lax.broadcasted_iota(jnp.int32, (C, C), 1)
    tri = jnp.where(ci >= cj, 1.0, 0.0).astype(BF16)
    pi = lax.broadcasted_iota(jnp.int32, (P, P), 0)
    pj = lax.broadcasted_iota(jnp.int32, (P, P), 1)
    same = (pi // C) == (pj // C)
    strict = jnp.logical_and(same, (pi % C) > (pj % C))
    incl = jnp.logical_and(same, (pi % C) >= (pj % C))
    eye = jnp.where(pi == pj, 1.0, 0.0).astype(F32)

    def stack(x):
        return jnp.concatenate([jnp.where(lane_lo, x, 0.0), jnp.where(lane_lo, 0.0, x)], axis=0)

    def chunk(c, carry):
        rows = pl.ds(pl.multiple_of(c * C, C), C)
        at, rt, kh, bh, vv, d_end = [], [], [], [], [], []
        for b in range(nbat):
            lw = lw_ref[b, rows, :]
            cum = _cumsum_rows(tri, lw)
            e_pos = jnp.exp(cum)
            e_neg = jnp.exp(-cum)
            d_end.append(e_pos[C - 1:C, :])
            rt.append(r_ref[b, rows, :] * e_pos)
            at.append(na_ref[b, rows, :] * jnp.exp(cum - lw))
            kh.append(k_ref[b, rows, :] * e_neg)
            bh.append(bb_ref[b, rows, :] * e_neg)
            vv.append(v_ref[b, rows, :])
        x2 = [jnp.concatenate([stack(at[b][:, sls[hp]]), stack(rt[b][:, sls[hp]])], axis=0).astype(BF16)
              for b, hp in streams]
        y2 = [jnp.concatenate([stack(kh[b][:, sls[hp]]), stack(bh[b][:, sls[hp]])], axis=0).astype(BF16)
              for b, hp in streams]
        vs = [stack(vv[b][:, sls[hp]]).astype(BF16) for b, hp in streams]
        gram = [_dot_nt(x2[i], y2[i]) for i in ids]
        s = [s_ref[i] for i in ids]
        xs = [_dot_nt(x2[i], s[i].astype(BF16)) for i in ids]
        a_ab = [jnp.where(strict, gram[i][:P, P:], 0.0) for i in ids]
        a_kr = [jnp.concatenate([jnp.where(strict, gram[i][:P, :P], 0.0),
                                 jnp.where(incl, gram[i][P:, :P], 0.0)], axis=0).astype(BF16) for i in ids]
        a_rb = [jnp.where(incl, gram[i][P:, P:], 0.0).astype(BF16) for i in ids]
        av = [_dot_nn(a_kr[i], vs[i]) for i in ids]
        inv = [eye + a_ab[i] for i in ids]
        q = [a_ab[i].astype(BF16) for i in ids]
        q = [_dot_nn(q[i], q[i]).astype(BF16) for i in ids]
        for _ in range(int(math.log2(C)) - 2):
            t = [_dot_nn(jnp.concatenate([q[i], inv[i].astype(BF16)], axis=0), q[i]) for i in ids]
            inv = [inv[i] + t[i][P:] for i in ids]
            q = [t[i][:P].astype(BF16) for i in ids]
        inv = [inv[i] + _dot_nn(inv[i].astype(BF16), q[i]) for i in ids]
        u = [_dot_nn(inv[i].astype(BF16), (xs[i][:P] + av[i][:P]).astype(BF16)).astype(BF16) for i in ids]
        yu = [_dot_nn(a_rb[i], u[i]) for i in ids]
        upd = [_dot_tn(jnp.concatenate([vs[i], u[i]], axis=0), y2[i]) for i in ids]
        for i, (b, hp) in enumerate(streams):
            ys = xs[i][P:] + av[i][P:] + yu[i]
            yacc_ref[b, rows, sls[hp]] = ys[:C] + ys[C:]
            s_ref[i] = s[i] * d_end[b][:, sls[hp]] + upd[i] * d_end[b][:, sls[hp]]
        return carry

    lax.fori_loop(0, sb // C, chunk, 0)

    li = lax.broadcasted_iota(jnp.int32, (LANES, LANES), 0)
    lj = lax.broadcasted_iota(jnp.int32, (LANES, LANES), 1)
    head_ones = jnp.where((li // RWKV_HEAD) == (lj // RWKV_HEAD), 1.0, 0.0).astype(BF16)

    def head_sums2(x):
        hi = x.astype(BF16)
        lo = (x - hi.astype(F32)).astype(BF16)
        return _dot_nn(hi, head_ones) + _dot_nn(lo, head_ones)

    mean = [head_sums2(yacc_ref[b, :, sls[hp]]) * (1.0 / RWKV_HEAD) for b, hp in streams]
    rks = [_dot_nn((r_ref[b, :, sls[hp]] * k_ref[b, :, sls[hp]] * rk_ref[:, sls[hp]]).astype(BF16), head_ones)
           for b, hp in streams]
    d = [yacc_ref[b, :, sls[hp]] - mean[i] for i, (b, hp) in enumerate(streams)]
    var = [head_sums2(d[i] * d[i]) * (1.0 / RWKV_HEAD) for i in ids]
    for i, (b, hp) in enumerate(streams):
        sl = sls[hp]
        yn = d[i] * lax.rsqrt(var[i] + GN_EPS) * gnw_ref[:, sl] + gnb_ref[:, sl]
        y_o[b, :, sl] = ((yn + rks[i] * v_ref[b, :, sl]) * g_ref[b, :, sl]).astype(BF16)


def _scan(r, k, v, na, bb, lw, g, gn_w, gn_b, r_k, *, batch, seq, sb, nbat):
    T, W = r.shape
    n_streams = nbat * (W // LANES)
    blk = lambda b, j: (b, j, 0)
    const = lambda b, j: (0, 0)
    in_specs = [pl.BlockSpec((nbat, sb, W), blk)] * 7 + [pl.BlockSpec((1, W), const)] * 3
    seq3 = lambda a: a.reshape(batch, seq, W)
    y = pl.pallas_call(
        functools.partial(_scan_kernel, sb=sb, nbat=nbat),
        grid=(batch // nbat, seq // sb),
        in_specs=in_specs,
        out_specs=pl.BlockSpec((nbat, sb, W), blk),
        out_shape=jax.ShapeDtypeStruct((batch, seq, W), BF16),
        scratch_shapes=[pltpu.VMEM((n_streams, LANES, LANES), F32), pltpu.VMEM((nbat, sb, W), F32)],
        compiler_params=pltpu.CompilerParams(
            dimension_semantics=("arbitrary", "arbitrary"), vmem_limit_bytes=VMEM_LIMIT),
        name="rwkv_scan",
    )(*(seq3(a) for a in (r, k, v, na, bb, lw, g)), gn_w, gn_b, r_k)
    return y.reshape(T, W)


def _tile(n, pref):
    return pref if n % pref == 0 else CHUNK


def kernel(x, ffn1_norm, ffn1_w1, ffn1_w3, ffn1_w2, mix_norm, w_in, mu_shift, w0, w_lora_up, a0, a_lora_up, g_lora_up, k_k, k_a, r_k, gn_w, gn_b, sgu_ln_g, sgu_ln_b, sgu_w, sgu_b, w_out, ffn2_norm, ffn2_w1, ffn2_w3, ffn2_w2, final_norm):
    B, S, D = x.shape
    T = B * S
    depth = ffn1_norm.shape[0]
    tm_ffn = _tile(S, 512)
    tm_mix = _tile(S, 256)
    sb = _tile(S, 256)
    rowv = lambda a: a.reshape(1, -1).astype(F32)

    xt = x.reshape(T, D)
    for l in range(depth):
        xt = _ffn(xt, rowv(ffn1_norm[l]), ffn1_w1[l].astype(BF16), ffn1_w3[l].astype(BF16),
                  ffn1_w2[l].astype(BF16), tm=tm_ffn)
        zeros = jnp.zeros((DECAY_LORA, RWKV_WIDTH), F32)
        lora = jnp.concatenate([jnp.concatenate([w_lora_up[l], zeros], axis=1),
                                jnp.concatenate([zeros, a_lora_up[l]], axis=1)], axis=0)
        sgu_bias = jnp.repeat(jnp.transpose(sgu_b[l]), GMLP_WIDTH // GMLP_GROUPS, axis=1)
        r, k, v, na, bb, lw, g, y_gmlp = _mixin(
            xt, rowv(mix_norm[l]), w_in[l].astype(BF16), rowv(mu_shift[l]), rowv(w0[l]), rowv(a0[l]),
            lora.astype(BF16), g_lora_up[l].astype(BF16), rowv(k_k[l]), rowv(k_a[l]),
            rowv(sgu_ln_g[l]), rowv(sgu_ln_b[l]), sgu_w[l], sgu_bias, tm=tm_mix, seq=S)
        y_rwkv = _scan(r, k, v, na, bb, lw, g, rowv(gn_w[l]), rowv(gn_b[l]), rowv(r_k[l]),
                       batch=B, seq=S, sb=sb, nbat=SCAN_BATCH if B % SCAN_BATCH == 0 else 1)
        last = l == depth - 1
        xt = _ffn(xt, rowv(ffn2_norm[l]), ffn2_w1[l].astype(BF16), ffn2_w3[l].astype(BF16),
                  ffn2_w2[l].astype(BF16), tm=tm_ffn,
                  mix_in=(y_rwkv, y_gmlp, w_out[l].astype(BF16)),
                  final_g=rowv(final_norm) if last else None)
    if depth == 0:
        raise ValueError("depth must be positive")
    return xt.reshape(B, S, D)
```

```python
import functools
import math

import jax
import jax.numpy as jnp
from jax import lax
from jax.experimental import pallas as pl
from jax.experimental.pallas import tpu as pltpu

F32 = jnp.float32
BF16 = jnp.bfloat16

RWKV_WIDTH = 512
RWKV_HEAD = 64
GMLP_WIDTH = 512
GMLP_GROUPS = 8
CHUNK = 128
DECAY_LORA = 64
AAA_LORA = 64
GATE_LORA = 128
D_SHIFT = 3 * RWKV_WIDTH + DECAY_LORA + AAA_LORA + GATE_LORA
RMS_EPS = 1e-6
GN_EPS = 64e-5
LN_EPS = 1e-5
FFN_RES_SCALE = 0.5

LANES = 128
SCAN_CHUNK = 64
SCAN_BATCH = 4
VMEM_LIMIT = 56 * 1024 * 1024


def _rmsnorm(x, g):
    return x * lax.rsqrt(jnp.mean(x * x, axis=-1, keepdims=True) + RMS_EPS) * g


def _gelu(x):
    return 0.5 * x * (1.0 + lax.erf(x * (1.0 / math.sqrt(2.0))))


def _sigmoid(x):
    return 1.0 / (1.0 + jnp.exp(-x))


def _bdot(a, b):
    return jnp.dot(a.astype(BF16), b.astype(BF16), preferred_element_type=F32)


def _ffn_kernel(*refs, mix, final):
    it = iter(refs)
    x_ref = next(it)
    if mix:
        yr_ref, yg_ref, wo_ref = next(it), next(it), next(it)
    g_ref, w1_ref, w3_ref, w2_ref = next(it), next(it), next(it), next(it)
    fg_ref = next(it) if final else None
    o_ref = next(it)

    x = x_ref[...]
    if mix:
        x = x + jnp.dot(yr_ref[...], wo_ref[:RWKV_WIDTH, :], preferred_element_type=F32)
        x = x + jnp.dot(yg_ref[...], wo_ref[RWKV_WIDTH:, :], preferred_element_type=F32)
    h = _rmsnorm(x, g_ref[...]).astype(BF16)
    a = jnp.dot(h, w1_ref[...], preferred_element_type=F32)
    b = jnp.dot(h, w3_ref[...], preferred_element_type=F32)
    z = (a * _sigmoid(a) * b).astype(BF16)
    y = x + FFN_RES_SCALE * jnp.dot(z, w2_ref[...], preferred_element_type=F32)
    if final:
        y = _rmsnorm(y, fg_ref[...])
    o_ref[...] = y


def _ffn(x, norm_g, w1, w3, w2, *, tm, mix_in=None, final_g=None):
    T, D = x.shape
    mix = mix_in is not None
    final = final_g is not None
    row = lambda i: (i, 0)
    whole = lambda a: pl.BlockSpec(a.shape, lambda i: (0, 0), pipeline_mode=pl.Buffered(1))
    in_specs = [pl.BlockSpec((tm, D), row)]
    args = [x]
    if mix:
        yr, yg, wo = mix_in
        in_specs += [pl.BlockSpec((tm, yr.shape[1]), row), pl.BlockSpec((tm, yg.shape[1]), row), whole(wo)]
        args += [yr, yg, wo]
    in_specs += [whole(norm_g), whole(w1), whole(w3), whole(w2)]
    args += [norm_g, w1, w3, w2]
    if final:
        in_specs.append(whole(final_g))
        args.append(final_g)
    return pl.pallas_call(
        functools.partial(_ffn_kernel, mix=mix, final=final),
        grid=(T // tm,),
        in_specs=in_specs,
        out_specs=pl.BlockSpec((tm, D), row),
        out_shape=jax.ShapeDtypeStruct((T, D), F32),
        compiler_params=pltpu.CompilerParams(
            dimension_semantics=("arbitrary",), vmem_limit_bytes=VMEM_LIMIT),
        name="ffn_mix_final" if mix else "ffn",
    )(*args)


def _head_sums(x, lane_lo):
    s0 = jnp.sum(jnp.where(lane_lo, x, 0.0), axis=-1, keepdims=True)
    s1 = jnp.sum(jnp.where(lane_lo, 0.0, x), axis=-1, keepdims=True)
    return jnp.where(lane_lo, s0, s1)


def _mixin_kernel(x_ref, g_ref, win_ref, mu_ref, w0_ref, a0_ref, lora_ref, gup_ref, kk_ref, ka_ref,
                  lng_ref, lnb_ref, sw_ref, sb_ref,
                  r_o, k_o, v_o, na_o, bb_o, lw_o, g_o, ygm_o, prev_ref, *, tm, tiles_per_seq):
    i = pl.program_id(0)
    W = RWKV_WIDTH
    h = _rmsnorm(x_ref[...], g_ref[...]).astype(BF16)
    p = jnp.dot(h, win_ref[...], preferred_element_type=F32)

    ps = p[:, :D_SHIFT]
    rolled = pltpu.roll(ps, 1, 0)
    first = (i % tiles_per_seq) == 0
    prev = jnp.where(first, 0.0, prev_ref[...])
    sub = 8
    row = lax.broadcasted_iota(jnp.int32, (sub, 1), 0)
    shifted = jnp.concatenate([jnp.where(row == 0, prev, rolled[:sub]), rolled[sub:]], axis=0)
    prev_ref[...] = ps[tm - 1:tm, :]
    ps = ps + (shifted - ps) * mu_ref[...]

    r = ps[:, :W]
    k = ps[:, W:2 * W]
    v = ps[:, 2 * W:3 * W]
    slab = ps[:, 3 * W:3 * W + DECAY_LORA + AAA_LORA]
    pg = ps[:, 3 * W + DECAY_LORA + AAA_LORA:]
    lane = lax.broadcasted_iota(jnp.int32, (1, LANES), 1)
    slab = jnp.where(lane < DECAY_LORA, jnp.tanh(slab), slab)
    lora = _bdot(slab, lora_ref[...])
    lw_o[...] = -math.exp(-0.5) * _sigmoid(w0_ref[...] + lora[:, :W])
    a = _sigmoid(a0_ref[...] + lora[:, W:])
    g_o[...] = _bdot(_sigmoid(pg), gup_ref[...])

    lane_lo = lane < RWKV_HEAD
    kk = k * kk_ref[...]
    for hp in range(W // LANES):
        sl = slice(hp * LANES, (hp + 1) * LANES)
        kkp = kk[:, sl]
        ss = _head_sums(kkp * kkp, lane_lo)
        kkn = kkp * jnp.minimum(lax.rsqrt(ss), 1e12)
        na_o[:, sl] = -kkn
        bb_o[:, sl] = kkn * a[:, sl]
    r_o[...] = r
    k_o[...] = k * (1.0 + (a - 1.0) * ka_ref[...])
    v_o[...] = v

    u = _gelu(p[:, D_SHIFT:D_SHIFT + GMLP_WIDTH])
    vg = _gelu(p[:, D_SHIFT + GMLP_WIDTH:])
    mean = jnp.mean(vg, axis=-1, keepdims=True)
    d = vg - mean
    var = jnp.mean(d * d, axis=-1, keepdims=True)
    vn = (d * lax.rsqrt(var + LN_EPS) * lng_ref[...] + lnb_ref[...]).astype(BF16)
    tr = lax.broadcasted_iota(jnp.int32, (CHUNK, CHUNK), 0)
    tc = lax.broadcasted_iota(jnp.int32, (CHUNK, CHUNK), 1)
    ws = [jnp.where(tr >= tc, sw_ref[gi], 0.0).astype(BF16) for gi in range(GMLP_GROUPS)]
    bias = sb_ref[...]
    for c in range(tm // CHUNK):
        rows = slice(c * CHUNK, (c + 1) * CHUNK)
        outs = []
        for hp in range(GMLP_WIDTH // LANES):
            vp = vn[rows, hp * LANES:(hp + 1) * LANES]
            m0 = jnp.dot(ws[2 * hp], vp, preferred_element_type=F32)
            m1 = jnp.dot(ws[2 * hp + 1], vp, preferred_element_type=F32)
            outs.append(jnp.where(lane_lo, m0, m1))
        mixed = jnp.concatenate(outs, axis=1) + bias
        ygm_o[rows, :] = (u[rows, :] * mixed).astype(BF16)


def _mixin(x1, mix_g, w_in, mu, w0, a0, lora, g_up, k_k, k_a, ln_g, ln_b, sgu_w, sgu_bias, *, tm, seq):
    T, D = x1.shape
    W = RWKV_WIDTH
    row = lambda i: (i, 0)
    const = lambda i: (0, 0)
    full = lambda a: pl.BlockSpec(a.shape, (lambda i: (0,) * a.ndim))
    in_specs = [pl.BlockSpec((tm, D), row)] + [full(a) for a in
                (mix_g, w_in, mu, w0, a0, lora, g_up, k_k, k_a, ln_g, ln_b, sgu_w, sgu_bias)]
    out_shape = [jax.ShapeDtypeStruct((T, W), F32)] * 7 + [jax.ShapeDtypeStruct((T, GMLP_WIDTH), BF16)]
    out_specs = [pl.BlockSpec((tm, W), row)] * 7 + [pl.BlockSpec((tm, GMLP_WIDTH), row)]
    return pl.pallas_call(
        functools.partial(_mixin_kernel, tm=tm, tiles_per_seq=seq // tm),
        grid=(T // tm,),
        in_specs=in_specs,
        out_specs=out_specs,
        out_shape=out_shape,
        scratch_shapes=[pltpu.VMEM((1, D_SHIFT), F32)],
        compiler_params=pltpu.CompilerParams(
            dimension_semantics=("arbitrary",), vmem_limit_bytes=VMEM_LIMIT),
        name="mixin",
    )(x1, mix_g, w_in, mu, w0, a0, lora, g_up, k_k, k_a, ln_g, ln_b, sgu_w, sgu_bias)


def _dot_nn(a, b):
    return jnp.dot(a, b, preferred_element_type=F32)


def _dot_nt(a, b):
    return lax.dot_general(a, b, (((1,), (1,)), ((), ())), preferred_element_type=F32)


def _dot_tn(a, b):
    return lax.dot_general(a, b, (((0,), (0,)), ((), ())), preferred_element_type=F32)


def _cumsum_rows(tri, x):
    out = None
    for _ in range(3):
        hi = x.astype(BF16)
        t = _dot_nn(tri, hi)
        out = t if out is None else out + t
        x = x - hi.astype(F32)
    return out


def _scan_kernel(r_ref, k_ref, v_ref, na_ref, bb_ref, lw_ref, g_ref, gnw_ref, gnb_ref, rk_ref,
                 y_o, s_ref, yacc_ref, *, sb, nbat):
    C = SCAN_CHUNK
    n_pairs = RWKV_WIDTH // LANES
    sls = [slice(hp * LANES, (hp + 1) * LANES) for hp in range(n_pairs)]
    streams = [(b, hp) for b in range(nbat) for hp in range(n_pairs)]
    ids = range(len(streams))

    @pl.when(pl.program_id(1) == 0)
    def _():
        s_ref[...] = jnp.zeros_like(s_ref)

    lane = lax.broadcasted_iota(jnp.int32, (1, LANES), 1)
    lane_lo = lane < RWKV_HEAD
    ci = lax.broadcasted_iota(jnp.int32, (C, C), 0)
    cj = lax.broadcasted_iota(jnp.int32, (C, C), 1)
    tri = jnp.where(ci >= cj, 1.0, 0.0).astype(BF16)
    pt = lax.broadcasted_iota(jnp.int32, (C, LANES), 0)
    ps_ = lax.broadcasted_iota(jnp.int32, (C, LANES), 1) % C
    strict = ps_ < pt
    incl = ps_ <= pt
    eye = jnp.where(ps_ == pt, 1.0, 0.0).astype(F32)
    li = lax.broadcasted_iota(jnp.int32, (LANES, LANES), 0)
    lj = lax.broadcasted_iota(jnp.int32, (LANES, LANES), 1)
    same_head = (li // RWKV_HEAD) == (lj // RWKV_HEAD)

    def stack(x):
        return jnp.concatenate([jnp.where(lane_lo, x, 0.0), jnp.where(lane_lo, 0.0, x)], axis=0).astype(BF16)

    def chunk(c, carry):
        rows = pl.ds(pl.multiple_of(c * C, C), C)
        at, rt, kh, bh, vv, d_end = [], [], [], [], [], []
        for b in range(nbat):
            lw = lw_ref[b, rows, :]
            cum = _cumsum_rows(tri, lw)
            e_pos = jnp.exp(cum)
            e_neg = jnp.exp(-cum)
            d_end.append(e_pos[C - 1:C, :])
            rt.append(r_ref[b, rows, :] * e_pos)
            at.append(na_ref[b, rows, :] * jnp.exp(cum - lw))
            kh.append(k_ref[b, rows, :] * e_neg)
            bh.append(bb_ref[b, rows, :] * e_neg)
            vv.append(v_ref[b, rows, :])
        x2 = [jnp.concatenate([at[b][:, sls[hp]], rt[b][:, sls[hp]]], axis=0).astype(BF16)
              for b, hp in streams]
        kb = [jnp.concatenate([kh[b][:, sls[hp]], bh[b][:, sls[hp]]], axis=0).astype(BF16)
              for b, hp in streams]
        kb_s = [jnp.concatenate([stack(kh[b][:, sls[hp]]), stack(bh[b][:, sls[hp]])], axis=0)
                for b, hp in streams]
        gram = [_dot_nt(x2[i], kb_s[i]) for i in ids]
        s = [s_ref[i] for i in ids]
        xs = [_dot_nt(x2[i], s[i].astype(BF16)) for i in ids]
        a_ab = [jnp.where(strict, gram[i][:C, LANES:], 0.0) for i in ids]
        a_kr = [jnp.concatenate([jnp.where(strict, gram[i][:C, :LANES], 0.0),
                                 jnp.where(incl, gram[i][C:, :LANES], 0.0)], axis=0).astype(BF16) for i in ids]
        a_rb = [jnp.where(incl, gram[i][C:, LANES:], 0.0).astype(BF16) for i in ids]
        av = [_dot_nn(a_kr[i], stack(vv[b][:, sls[hp]])) for i, (b, hp) in enumerate(streams)]
        inv = [eye + a_ab[i] for i in ids]
        q = [_dot_nn(a_ab[i].astype(BF16), stack(a_ab[i])) for i in ids]
        for _ in range(int(math.log2(C)) - 2):
            t = [_dot_nn(jnp.concatenate([q[i], inv[i]], axis=0).astype(BF16), stack(q[i])) for i in ids]
            inv = [inv[i] + t[i][C:] for i in ids]
            q = [t[i][:C] for i in ids]
        inv = [inv[i] + _dot_nn(inv[i].astype(BF16), stack(q[i])) for i in ids]
        u = [_dot_nn(inv[i].astype(BF16), stack(xs[i][:C] + av[i][:C])) for i in ids]
        yu = [_dot_nn(a_rb[i], stack(u[i])) for i in ids]
        upd = [_dot_tn(jnp.concatenate([vv[b][:, sls[hp]], u[i]], axis=0).astype(BF16), kb[i])
               for i, (b, hp) in enumerate(streams)]
        for i, (b, hp) in enumerate(streams):
            yacc_ref[b, rows, sls[hp]] = xs[i][C:] + av[i][C:] + yu[i]
            s_ref[i] = (s[i] + jnp.where(same_head, upd[i], 0.0)) * d_end[b][:, sls[hp]]
        return carry

    lax.fori_loop(0, sb // C, chunk, 0)

    head_ones = jnp.where(same_head, 1.0, 0.0).astype(BF16)

    def head_sums2(x):
        hi = x.astype(BF16)
        lo = (x - hi.astype(F32)).astype(BF16)
        return _dot_nn(hi, head_ones) + _dot_nn(lo, head_ones)

    mean = [head_sums2(yacc_ref[b, :, sls[hp]]) * (1.0 / RWKV_HEAD) for b, hp in streams]
    rks = [_dot_nn((r_ref[b, :, sls[hp]] * k_ref[b, :, sls[hp]] * rk_ref[:, sls[hp]]).astype(BF16), head_ones)
           for b, hp in streams]
    d = [yacc_ref[b, :, sls[hp]] - mean[i] for i, (b, hp) in enumerate(streams)]
    var = [_dot_nn((d[i] * d[i]).astype(BF16), head_ones) * (1.0 / RWKV_HEAD) for i in ids]
    for i, (b, hp) in enumerate(streams):
        sl = sls[hp]
        yn = d[i] * lax.rsqrt(var[i] + GN_EPS) * gnw_ref[:, sl] + gnb_ref[:, sl]
        y_o[b, :, sl] = ((yn + rks[i] * v_ref[b, :, sl]) * g_ref[b, :, sl]).astype(BF16)


def _scan(r, k, v, na, bb, lw, g, gn_w, gn_b, r_k, *, batch, seq, sb, nbat):
    T, W = r.shape
    n_streams = nbat * (W // LANES)
    blk = lambda b, j: (b, j, 0)
    const = lambda b, j: (0, 0)
    in_specs = [pl.BlockSpec((nbat, sb, W), blk)] * 7 + [pl.BlockSpec((1, W), const)] * 3
    seq3 = lambda a: a.reshape(batch, seq, W)
    y = pl.pallas_call(
        functools.partial(_scan_kernel, sb=sb, nbat=nbat),
        grid=(batch // nbat, seq // sb),
        in_specs=in_specs,
        out_specs=pl.BlockSpec((nbat, sb, W), blk),
        out_shape=jax.ShapeDtypeStruct((batch, seq, W), BF16),
        scratch_shapes=[pltpu.VMEM((n_streams, LANES, LANES), F32), pltpu.VMEM((nbat, sb, W), F32)],
        compiler_params=pltpu.CompilerParams(
            dimension_semantics=("arbitrary", "arbitrary"), vmem_limit_bytes=VMEM_LIMIT),
        name="rwkv_scan",
    )(*(seq3(a) for a in (r, k, v, na, bb, lw, g)), gn_w, gn_b, r_k)
    return y.reshape(T, W)


def _tile(n, pref):
    return pref if n % pref == 0 else CHUNK


def kernel(x, ffn1_norm, ffn1_w1, ffn1_w3, ffn1_w2, mix_norm, w_in, mu_shift, w0, w_lora_up, a0, a_lora_up, g_lora_up, k_k, k_a, r_k, gn_w, gn_b, sgu_ln_g, sgu_ln_b, sgu_w, sgu_b, w_out, ffn2_norm, ffn2_w1, ffn2_w3, ffn2_w2, final_norm):
    B, S, D = x.shape
    T = B * S
    depth = ffn1_norm.shape[0]
    tm_ffn = _tile(S, 512)
    tm_mix = _tile(S, 256)
    sb = _tile(S, 256)
    rowv = lambda a: a.reshape(1, -1).astype(F32)

    xt = x.reshape(T, D)
    for l in range(depth):
        xt = _ffn(xt, rowv(ffn1_norm[l]), ffn1_w1[l].astype(BF16), ffn1_w3[l].astype(BF16),
                  ffn1_w2[l].astype(BF16), tm=tm_ffn)
        zeros = jnp.zeros((DECAY_LORA, RWKV_WIDTH), F32)
        lora = jnp.concatenate([jnp.concatenate([w_lora_up[l], zeros], axis=1),
                                jnp.concatenate([zeros, a_lora_up[l]], axis=1)], axis=0)
        sgu_bias = jnp.repeat(jnp.transpose(sgu_b[l]), GMLP_WIDTH // GMLP_GROUPS, axis=1)
        r, k, v, na, bb, lw, g, y_gmlp = _mixin(
            xt, rowv(mix_norm[l]), w_in[l].astype(BF16), rowv(mu_shift[l]), rowv(w0[l]), rowv(a0[l]),
            lora.astype(BF16), g_lora_up[l].astype(BF16), rowv(k_k[l]), rowv(k_a[l]),
            rowv(sgu_ln_g[l]), rowv(sgu_ln_b[l]), sgu_w[l], sgu_bias, tm=tm_mix, seq=S)
        y_rwkv = _scan(r, k, v, na, bb, lw, g, rowv(gn_w[l]), rowv(gn_b[l]), rowv(r_k[l]),
                       batch=B, seq=S, sb=sb, nbat=SCAN_BATCH if B % SCAN_BATCH == 0 else 1)
        last = l == depth - 1
        xt = _ffn(xt, rowv(ffn2_norm[l]), ffn2_w1[l].astype(BF16), ffn2_w3[l].astype(BF16),
                  ffn2_w2[l].astype(BF16), tm=tm_ffn,
                  mix_in=(y_rwkv, y_gmlp, w_out[l].astype(BF16)),
                  final_g=rowv(final_norm) if last else None)
    if depth == 0:
        raise ValueError("depth must be positive")
    return xt.reshape(B, S, D)
```

```python
import functools
import math

import jax
import jax.numpy as jnp
from jax import lax
from jax.experimental import pallas as pl
from jax.experimental.pallas import tpu as pltpu

F32 = jnp.float32
BF16 = jnp.bfloat16

RWKV_WIDTH = 512
RWKV_HEAD = 64
GMLP_WIDTH = 512
GMLP_GROUPS = 8
CHUNK = 128
DECAY_LORA = 64
AAA_LORA = 64
GATE_LORA = 128
D_SHIFT = 3 * RWKV_WIDTH + DECAY_LORA + AAA_LORA + GATE_LORA
RMS_EPS = 1e-6
GN_EPS = 64e-5
LN_EPS = 1e-5
FFN_RES_SCALE = 0.5

LANES = 128
SCAN_CHUNK = 64
SCAN_BATCH = 4
VMEM_LIMIT = 56 * 1024 * 1024


def _rmsnorm(x, g):
    return x * lax.rsqrt(jnp.mean(x * x, axis=-1, keepdims=True) + RMS_EPS) * g


def _gelu(x):
    return 0.5 * x * (1.0 + lax.erf(x * (1.0 / math.sqrt(2.0))))


def _sigmoid(x):
    return 1.0 / (1.0 + jnp.exp(-x))


def _bdot(a, b):
    return jnp.dot(a.astype(BF16), b.astype(BF16), preferred_element_type=F32)


def _ffn_kernel(*refs, mix, final):
    it = iter(refs)
    x_ref = next(it)
    if mix:
        yr_ref, yg_ref, wo_ref = next(it), next(it), next(it)
    g_ref, w1_ref, w3_ref, w2_ref = next(it), next(it), next(it), next(it)
    fg_ref = next(it) if final else None
    o_ref = next(it)

    x = x_ref[...]
    if mix:
        x = x + jnp.dot(yr_ref[...], wo_ref[:RWKV_WIDTH, :], preferred_element_type=F32)
        x = x + jnp.dot(yg_ref[...], wo_ref[RWKV_WIDTH:, :], preferred_element_type=F32)
    h = _rmsnorm(x, g_ref[...]).astype(BF16)
    a = jnp.dot(h, w1_ref[...], preferred_element_type=F32)
    b = jnp.dot(h, w3_ref[...], preferred_element_type=F32)
    z = (a * _sigmoid(a) * b).astype(BF16)
    y = x + FFN_RES_SCALE * jnp.dot(z, w2_ref[...], preferred_element_type=F32)
    if final:
        y = _rmsnorm(y, fg_ref[...])
    o_ref[...] = y


def _ffn(x, norm_g, w1, w3, w2, *, tm, mix_in=None, final_g=None):
    T, D = x.shape
    mix = mix_in is not None
    final = final_g is not None
    row = lambda i: (i, 0)
    whole = lambda a: pl.BlockSpec(a.shape, lambda i: (0, 0), pipeline_mode=pl.Buffered(1))
    in_specs = [pl.BlockSpec((tm, D), row)]
    args = [x]
    if mix:
        yr, yg, wo = mix_in
        in_specs += [pl.BlockSpec((tm, yr.shape[1]), row), pl.BlockSpec((tm, yg.shape[1]), row), whole(wo)]
        args += [yr, yg, wo]
    in_specs += [whole(norm_g), whole(w1), whole(w3), whole(w2)]
    args += [norm_g, w1, w3, w2]
    if final:
        in_specs.append(whole(final_g))
        args.append(final_g)
    return pl.pallas_call(
        functools.partial(_ffn_kernel, mix=mix, final=final),
        grid=(T // tm,),
        in_specs=in_specs,
        out_specs=pl.BlockSpec((tm, D), row),
        out_shape=jax.ShapeDtypeStruct((T, D), F32),
        compiler_params=pltpu.CompilerParams(
            dimension_semantics=("arbitrary",), vmem_limit_bytes=VMEM_LIMIT),
        name="ffn_mix_final" if mix else "ffn",
    )(*args)


def _head_sums(x, lane_lo):
    s0 = jnp.sum(jnp.where(lane_lo, x, 0.0), axis=-1, keepdims=True)
    s1 = jnp.sum(jnp.where(lane_lo, 0.0, x), axis=-1, keepdims=True)
    return jnp.where(lane_lo, s0, s1)


def _mixin_kernel(x_ref, g_ref, win_ref, mu_ref, w0_ref, a0_ref, lora_ref, gup_ref, kk_ref, ka_ref,
                  lng_ref, lnb_ref, sw_ref, sb_ref,
                  r_o, k_o, v_o, na_o, bb_o, lw_o, g_o, ygm_o, prev_ref, *, tm, tiles_per_seq):
    i = pl.program_id(0)
    W = RWKV_WIDTH
    h = _rmsnorm(x_ref[...], g_ref[...]).astype(BF16)
    p = jnp.dot(h, win_ref[...], preferred_element_type=F32)

    ps = p[:, :D_SHIFT]
    rolled = pltpu.roll(ps, 1, 0)
    first = (i % tiles_per_seq) == 0
    prev = jnp.where(first, 0.0, prev_ref[...])
    sub = 8
    row = lax.broadcasted_iota(jnp.int32, (sub, 1), 0)
    shifted = jnp.concatenate([jnp.where(row == 0, prev, rolled[:sub]), rolled[sub:]], axis=0)
    prev_ref[...] = ps[tm - 1:tm, :]
    ps = ps + (shifted - ps) * mu_ref[...]

    r = ps[:, :W]
    k = ps[:, W:2 * W]
    v = ps[:, 2 * W:3 * W]
    slab = ps[:, 3 * W:3 * W + DECAY_LORA + AAA_LORA]
    pg = ps[:, 3 * W + DECAY_LORA + AAA_LORA:]
    lane = lax.broadcasted_iota(jnp.int32, (1, LANES), 1)
    slab = jnp.where(lane < DECAY_LORA, jnp.tanh(slab), slab)
    lora = _bdot(slab, lora_ref[...])
    lw_o[...] = -math.exp(-0.5) * _sigmoid(w0_ref[...] + lora[:, :W])
    a = _sigmoid(a0_ref[...] + lora[:, W:])
    g_o[...] = _bdot(_sigmoid(pg), gup_ref[...])

    lane_lo = lane < RWKV_HEAD
    kk = k * kk_ref[...]
    for hp in range(W // LANES):
        sl = slice(hp * LANES, (hp + 1) * LANES)
        kkp = kk[:, sl]
        ss = _head_sums(kkp * kkp, lane_lo)
        kkn = kkp * jnp.minimum(lax.rsqrt(ss), 1e12)
        na_o[:, sl] = -kkn
        bb_o[:, sl] = kkn * a[:, sl]
    r_o[...] = r
    k_o[...] = k * (1.0 + (a - 1.0) * ka_ref[...])
    v_o[...] = v

    u = _gelu(p[:, D_SHIFT:D_SHIFT + GMLP_WIDTH])
    vg = _gelu(p[:, D_SHIFT + GMLP_WIDTH:])
    mean = jnp.mean(vg, axis=-1, keepdims=True)
    d = vg - mean
    var = jnp.mean(d * d, axis=-1, keepdims=True)
    vn = (d * lax.rsqrt(var + LN_EPS) * lng_ref[...] + lnb_ref[...]).astype(BF16)
    tr = lax.broadcasted_iota(jnp.int32, (CHUNK, CHUNK), 0)
    tc = lax.broadcasted_iota(jnp.int32, (CHUNK, CHUNK), 1)
    ws = [jnp.where(tr >= tc, sw_ref[gi], 0.0).astype(BF16) for gi in range(GMLP_GROUPS)]
    bias = sb_ref[...]
    for c in range(tm // CHUNK):
        rows = slice(c * CHUNK, (c + 1) * CHUNK)
        outs = []
        for hp in range(GMLP_WIDTH // LANES):
            vp = vn[rows, hp * LANES:(hp + 1) * LANES]
            m0 = jnp.dot(ws[2 * hp], vp, preferred_element_type=F32)
            m1 = jnp.dot(ws[2 * hp + 1], vp, preferred_element_type=F32)
            outs.append(jnp.where(lane_lo, m0, m1))
        mixed = jnp.concatenate(outs, axis=1) + bias
        ygm_o[rows, :] = (u[rows, :] * mixed).astype(BF16)


def _mixin(x1, mix_g, w_in, mu, w0, a0, lora, g_up, k_k, k_a, ln_g, ln_b, sgu_w, sgu_bias, *, tm, seq):
    T, D = x1.shape
    W = RWKV_WIDTH
    row = lambda i: (i, 0)
    const = lambda i: (0, 0)
    full = lambda a: pl.BlockSpec(a.shape, (lambda i: (0,) * a.ndim), pipeline_mode=pl.Buffered(1))
    in_specs = [pl.BlockSpec((tm, D), row)] + [full(a) for a in
                (mix_g, w_in, mu, w0, a0, lora, g_up, k_k, k_a, ln_g, ln_b, sgu_w, sgu_bias)]
    out_shape = [jax.ShapeDtypeStruct((T, W), F32)] * 7 + [jax.ShapeDtypeStruct((T, GMLP_WIDTH), BF16)]
    out_specs = [pl.BlockSpec((tm, W), row)] * 7 + [pl.BlockSpec((tm, GMLP_WIDTH), row)]
    return pl.pallas_call(
        functools.partial(_mixin_kernel, tm=tm, tiles_per_seq=seq // tm),
        grid=(T // tm,),
        in_specs=in_specs,
        out_specs=out_specs,
        out_shape=out_shape,
        scratch_shapes=[pltpu.VMEM((1, D_SHIFT), F32)],
        compiler_params=pltpu.CompilerParams(
            dimension_semantics=("arbitrary",), vmem_limit_bytes=VMEM_LIMIT),
        name="mixin",
    )(x1, mix_g, w_in, mu, w0, a0, lora, g_up, k_k, k_a, ln_g, ln_b, sgu_w, sgu_bias)


def _dot_nn(a, b):
    return jnp.dot(a, b, preferred_element_type=F32)


def _dot_nt(a, b):
    return lax.dot_general(a, b, (((1,), (1,)), ((), ())), preferred_element_type=F32)


def _dot_tn(a, b):
    return lax.dot_general(a, b, (((0,), (0,)), ((), ())), preferred_element_type=F32)


def _cumsum_rows(tri, x):
    out = None
    for _ in range(3):
        hi = x.astype(BF16)
        t = _dot_nn(tri, hi)
        out = t if out is None else out + t
        x = x - hi.astype(F32)
    return out


def _scan_kernel(r_ref, k_ref, v_ref, na_ref, bb_ref, lw_ref, g_ref, gnw_ref, gnb_ref, rk_ref,
                 y_o, s_ref, yacc_ref, *, sb, nbat):
    C = SCAN_CHUNK
    n_pairs = RWKV_WIDTH // LANES
    sls = [slice(hp * LANES, (hp + 1) * LANES) for hp in range(n_pairs)]
    streams = [(b, hp) for b in range(nbat) for hp in range(n_pairs)]
    ids = range(len(streams))

    @pl.when(pl.program_id(1) == 0)
    def _():
        s_ref[...] = jnp.zeros_like(s_ref)

    lane = lax.broadcasted_iota(jnp.int32, (1, LANES), 1)
    lane_lo = lane < RWKV_HEAD
    ci = lax.broadcasted_iota(jnp.int32, (C, C), 0)
    cj = lax.broadcasted_iota(jnp.int32, (C, C), 1)
    tri = jnp.where(ci >= cj, 1.0, 0.0).astype(BF16)
    pt = lax.broadcasted_iota(jnp.int32, (C, LANES), 0)
    ps_ = lax.broadcasted_iota(jnp.int32, (C, LANES), 1) % C
    strict = ps_ < pt
    incl = ps_ <= pt
    eye = jnp.where(ps_ == pt, 1.0, 0.0).astype(F32)
    li = lax.broadcasted_iota(jnp.int32, (LANES, LANES), 0)
    lj = lax.broadcasted_iota(jnp.int32, (LANES, LANES), 1)
    same_head = (li // RWKV_HEAD) == (lj // RWKV_HEAD)

    def stack(x):
        return jnp.concatenate([jnp.where(lane_lo, x, 0.0), jnp.where(lane_lo, 0.0, x)], axis=0).astype(BF16)

    def prep(c):
        rows = slice(c * C, (c + 1) * C)
        x2, kb, kb_s, vs, d_end = [], [], [], [], []
        for b in range(nbat):
            lw = lw_ref[b, rows, :]
            cum = _cumsum_rows(tri, lw)
            e_pos = jnp.exp(cum)
            e_neg = jnp.exp(-cum)
            rt = r_ref[b, rows, :] * e_pos
            at = na_ref[b, rows, :] * jnp.exp(cum - lw)
            kh = k_ref[b, rows, :] * e_neg
            bh = bb_ref[b, rows, :] * e_neg
            vv = v_ref[b, rows, :]
            for sl in sls:
                x2.append(jnp.concatenate([at[:, sl], rt[:, sl]], axis=0).astype(BF16))
                kb.append(jnp.concatenate([kh[:, sl], bh[:, sl]], axis=0).astype(BF16))
                kb_s.append(jnp.concatenate([stack(kh[:, sl]), stack(bh[:, sl])], axis=0))
                vs.append((vv[:, sl].astype(BF16), stack(vv[:, sl])))
                d_end.append(e_pos[C - 1:C, sl])
        return x2, kb, kb_s, vs, d_end

    def head(ops, s):
        x2, kb, kb_s, vs, d_end = ops
        gram = [_dot_nt(x2[i], kb_s[i]) for i in ids]
        xs = [_dot_nt(x2[i], s[i].astype(BF16)) for i in ids]
        a_ab = [jnp.where(strict, gram[i][:C, LANES:], 0.0) for i in ids]
        a_kr = [jnp.concatenate([jnp.where(strict, gram[i][:C, :LANES], 0.0),
                                 jnp.where(incl, gram[i][C:, :LANES], 0.0)], axis=0).astype(BF16) for i in ids]
        a_rb = [jnp.where(incl, gram[i][C:, LANES:], 0.0).astype(BF16) for i in ids]
        av = [_dot_nn(a_kr[i], vs[i][1]) for i in ids]
        inv = [eye + a_ab[i] for i in ids]
        q = [_dot_nn(a_ab[i].astype(BF16), stack(a_ab[i])) for i in ids]
        for _ in range(int(math.log2(C)) - 2):
            t = [_dot_nn(jnp.concatenate([q[i], inv[i]], axis=0).astype(BF16), stack(q[i])) for i in ids]
            inv = [inv[i] + t[i][C:] for i in ids]
            q = [t[i][:C] for i in ids]
        inv = [inv[i] + _dot_nn(inv[i].astype(BF16), stack(q[i])) for i in ids]
        return inv, a_rb, xs, av

    def tail(c, ops, hd, s):
        x2, kb, kb_s, vs, d_end = ops
        inv, a_rb, xs, av = hd
        rows = slice(c * C, (c + 1) * C)
        u = [_dot_nn(inv[i].astype(BF16), stack(xs[i][:C] + av[i][:C])) for i in ids]
        yu = [_dot_nn(a_rb[i], stack(u[i])) for i in ids]
        upd = [_dot_tn(jnp.concatenate([vs[i][0], u[i].astype(BF16)], axis=0), kb[i])
               for i in ids]
        for i, (b, hp) in enumerate(streams):
            yacc_ref[b, rows, sls[hp]] = xs[i][C:] + av[i][C:] + yu[i]
        return [(s[i] + jnp.where(same_head, upd[i], 0.0)) * d_end[i] for i in ids]

    n_chunks = sb // C
    state = [s_ref[i] for i in ids]
    ops = prep(0)
    for c in range(n_chunks):
        hd = head(ops, state)
        nxt = prep(c + 1) if c + 1 < n_chunks else None
        state = tail(c, ops, hd, state)
        ops = nxt
    for i in ids:
        s_ref[i] = state[i]

    head_ones = jnp.where(same_head, 1.0, 0.0).astype(BF16)

    def head_sums2(x):
        hi = x.astype(BF16)
        lo = (x - hi.astype(F32)).astype(BF16)
        return _dot_nn(hi, head_ones) + _dot_nn(lo, head_ones)

    mean = [head_sums2(yacc_ref[b, :, sls[hp]]) * (1.0 / RWKV_HEAD) for b, hp in streams]
    rks = [_dot_nn((r_ref[b, :, sls[hp]] * k_ref[b, :, sls[hp]] * rk_ref[:, sls[hp]]).astype(BF16), head_ones)
           for b, hp in streams]
    d = [yacc_ref[b, :, sls[hp]] - mean[i] for i, (b, hp) in enumerate(streams)]
    var = [_dot_nn((d[i] * d[i]).astype(BF16), head_ones) * (1.0 / RWKV_HEAD) for i in ids]
    for i, (b, hp) in enumerate(streams):
        sl = sls[hp]
        yn = d[i] * lax.rsqrt(var[i] + GN_EPS) * gnw_ref[:, sl] + gnb_ref[:, sl]
        y_o[b, :, sl] = ((yn + rks[i] * v_ref[b, :, sl]) * g_ref[b, :, sl]).astype(BF16)


def _scan(r, k, v, na, bb, lw, g, gn_w, gn_b, r_k, *, batch, seq, sb, nbat):
    T, W = r.shape
    n_streams = nbat * (W // LANES)
    blk = lambda b, j: (b, j, 0)
    const = lambda b, j: (0, 0)
    in_specs = [pl.BlockSpec((nbat, sb, W), blk)] * 7 + [pl.BlockSpec((1, W), const)] * 3
    seq3 = lambda a: a.reshape(batch, seq, W)
    y = pl.pallas_call(
        functools.partial(_scan_kernel, sb=sb, nbat=nbat),
        grid=(batch // nbat, seq // sb),
        in_specs=in_specs,
        out_specs=pl.BlockSpec((nbat, sb, W), blk),
        out_shape=jax.ShapeDtypeStruct((batch, seq, W), BF16),
        scratch_shapes=[pltpu.VMEM((n_streams, LANES, LANES), F32), pltpu.VMEM((nbat, sb, W), F32)],
        compiler_params=pltpu.CompilerParams(
            dimension_semantics=("arbitrary", "arbitrary"), vmem_limit_bytes=VMEM_LIMIT),
        name="rwkv_scan",
    )(*(seq3(a) for a in (r, k, v, na, bb, lw, g)), gn_w, gn_b, r_k)
    return y.reshape(T, W)


def _tile(n, pref):
    return pref if n % pref == 0 else CHUNK


def kernel(x, ffn1_norm, ffn1_w1, ffn1_w3, ffn1_w2, mix_norm, w_in, mu_shift, w0, w_lora_up, a0, a_lora_up, g_lora_up, k_k, k_a, r_k, gn_w, gn_b, sgu_ln_g, sgu_ln_b, sgu_w, sgu_b, w_out, ffn2_norm, ffn2_w1, ffn2_w3, ffn2_w2, final_norm):
    B, S, D = x.shape
    T = B * S
    depth = ffn1_norm.shape[0]
    tm_ffn = _tile(S, 512)
    tm_mix = _tile(S, 512)
    sb = _tile(S, 256)
    rowv = lambda a: a.reshape(1, -1).astype(F32)

    xt = x.reshape(T, D)
    for l in range(depth):
        xt = _ffn(xt, rowv(ffn1_norm[l]), ffn1_w1[l].astype(BF16), ffn1_w3[l].astype(BF16),
                  ffn1_w2[l].astype(BF16), tm=tm_ffn)
        zeros = jnp.zeros((DECAY_LORA, RWKV_WIDTH), F32)
        lora = jnp.concatenate([jnp.concatenate([w_lora_up[l], zeros], axis=1),
                                jnp.concatenate([zeros, a_lora_up[l]], axis=1)], axis=0)
        sgu_bias = jnp.repeat(jnp.transpose(sgu_b[l]), GMLP_WIDTH // GMLP_GROUPS, axis=1)
        r, k, v, na, bb, lw, g, y_gmlp = _mixin(
            xt, rowv(mix_norm[l]), w_in[l].astype(BF16), rowv(mu_shift[l]), rowv(w0[l]), rowv(a0[l]),
            lora.astype(BF16), g_lora_up[l].astype(BF16), rowv(k_k[l]), rowv(k_a[l]),
            rowv(sgu_ln_g[l]), rowv(sgu_ln_b[l]), sgu_w[l], sgu_bias, tm=tm_mix, seq=S)
        y_rwkv = _scan(r, k, v, na, bb, lw, g, rowv(gn_w[l]), rowv(gn_b[l]), rowv(r_k[l]),
                       batch=B, seq=S, sb=sb, nbat=SCAN_BATCH if B % SCAN_BATCH == 0 else 1)
        last = l == depth - 1
        xt = _ffn(xt, rowv(ffn2_norm[l]), ffn2_w1[l].astype(BF16), ffn2_w3[l].astype(BF16),
                  ffn2_w2[l].astype(BF16), tm=tm_ffn,
                  mix_in=(y_rwkv, y_gmlp, w_out[l].astype(BF16)),
                  final_g=rowv(final_norm) if last else None)
    if depth == 0:
        raise ValueError("depth must be positive")
    return xt.reshape(B, S, D)
```

```python
import functools
import math

import jax
import jax.numpy as jnp
from jax import lax
from jax.experimental import pallas as pl
from jax.experimental.pallas import tpu as pltpu

F32 = jnp.float32
BF16 = jnp.bfloat16

RWKV_WIDTH = 512
RWKV_HEAD = 64
GMLP_WIDTH = 512
GMLP_GROUPS = 8
CHUNK = 128
DECAY_LORA = 64
AAA_LORA = 64
GATE_LORA = 128
D_SHIFT = 3 * RWKV_WIDTH + DECAY_LORA + AAA_LORA + GATE_LORA
RMS_EPS = 1e-6
GN_EPS = 64e-5
LN_EPS = 1e-5
FFN_RES_SCALE = 0.5

LANES = 128
BF16_ROWS = 16
NARROW_ROWS = 128
SCAN_CHUNK = 64
SCAN_BATCH = 4
VMEM_LIMIT = 56 * 1024 * 1024


def _rmsnorm(x, g):
    return x * lax.rsqrt(jnp.mean(x * x, axis=-1, keepdims=True) + RMS_EPS) * g


def _gelu(x):
    return 0.5 * x * (1.0 + lax.erf(x * (1.0 / math.sqrt(2.0))))


def _sigmoid(x):
    return 1.0 / (1.0 + jnp.exp(-x))


def _bdot(a, b):
    return jnp.dot(a.astype(BF16), b.astype(BF16), preferred_element_type=F32)


def _ffn_kernel(*refs, mix, final, n_cast):
    it = iter(refs)
    x_ref = next(it)
    if mix:
        yr_ref, yg_ref, wo_ref = next(it), next(it), next(it)
    g_ref, w1_ref, w3_ref, w2_ref = next(it), next(it), next(it), next(it)
    fg_ref = next(it) if final else None
    cast_in = [next(it) for _ in range(n_cast)]
    o_ref = next(it)
    cast_out = [next(it) for _ in range(n_cast)]

    x = x_ref[...]
    if mix:
        x = x + jnp.dot(yr_ref[...], wo_ref[:RWKV_WIDTH, :], preferred_element_type=F32)
        x = x + jnp.dot(yg_ref[...], wo_ref[RWKV_WIDTH:, :], preferred_element_type=F32)
    h = _rmsnorm(x, g_ref[...]).astype(BF16)
    a = jnp.dot(h, w1_ref[...], preferred_element_type=F32)
    b = jnp.dot(h, w3_ref[...], preferred_element_type=F32)
    z = (a * _sigmoid(a) * b).astype(BF16)
    y = x + FFN_RES_SCALE * jnp.dot(z, w2_ref[...], preferred_element_type=F32)
    if final:
        y = _rmsnorm(y, fg_ref[...])
    o_ref[...] = y
    for src, dst in zip(cast_in, cast_out):
        dst[...] = src[...].astype(BF16)


def _narrow_kernel(*refs):
    half = len(refs) // 2
    for src, dst in zip(refs[:half], refs[half:]):
        dst[...] = src[...].astype(BF16)


def _narrow(arrs, *, rows):
    specs = [pl.BlockSpec((rows, a.shape[1]), lambda i: (i, 0)) for a in arrs]
    return pl.pallas_call(
        _narrow_kernel,
        grid=(arrs[0].shape[0] // rows,),
        in_specs=specs,
        out_specs=specs,
        out_shape=[jax.ShapeDtypeStruct(a.shape, BF16) for a in arrs],
        compiler_params=pltpu.CompilerParams(dimension_semantics=("arbitrary",), vmem_limit_bytes=VMEM_LIMIT),
        name="narrow_weights",
    )(*arrs)


def _ffn(x, norm_g, w1, w3, w2, *, tm, mix_in=None, final_g=None, narrow=()):
    T, D = x.shape
    n = T // tm
    mix = mix_in is not None
    final = final_g is not None
    row = lambda i: (i, 0)
    whole = lambda a: pl.BlockSpec(a.shape, lambda i: (0, 0), pipeline_mode=pl.Buffered(1))
    in_specs = [pl.BlockSpec((tm, D), row)]
    args = [x]
    if mix:
        yr, yg, wo = mix_in
        in_specs += [pl.BlockSpec((tm, yr.shape[1]), row), pl.BlockSpec((tm, yg.shape[1]), row), whole(wo)]
        args += [yr, yg, wo]
    in_specs += [whole(norm_g), whole(w1), whole(w3), whole(w2)]
    args += [norm_g, w1, w3, w2]
    if final:
        in_specs.append(whole(final_g))
        args.append(final_g)
    out_specs = [pl.BlockSpec((tm, D), row)]
    out_shape = [jax.ShapeDtypeStruct((T, D), F32)]
    for a in narrow:
        blk = pl.BlockSpec((a.shape[0] // n, a.shape[1]), row)
        in_specs.append(blk)
        args.append(a)
        out_specs.append(blk)
        out_shape.append(jax.ShapeDtypeStruct(a.shape, BF16))
    outs = pl.pallas_call(
        functools.partial(_ffn_kernel, mix=mix, final=final, n_cast=len(narrow)),
        grid=(n,),
        in_specs=in_specs,
        out_specs=out_specs,
        out_shape=out_shape,
        compiler_params=pltpu.CompilerParams(
            dimension_semantics=("arbitrary",), vmem_limit_bytes=VMEM_LIMIT),
        name="ffn_mix_final" if mix else "ffn",
    )(*args)
    return outs[0], list(outs[1:])


def _head_sums(x, lane_lo):
    s0 = jnp.sum(jnp.where(lane_lo, x, 0.0), axis=-1, keepdims=True)
    s1 = jnp.sum(jnp.where(lane_lo, 0.0, x), axis=-1, keepdims=True)
    return jnp.where(lane_lo, s0, s1)


def _mixin_kernel(x_ref, g_ref, win_ref, mu_ref, w0_ref, a0_ref, lora_ref, gup_ref, kk_ref, ka_ref,
                  lng_ref, lnb_ref, sw_ref, sb_ref,
                  r_o, k_o, v_o, na_o, bb_o, lw_o, g_o, ygm_o, prev_ref, *, tm, tiles_per_seq):
    i = pl.program_id(0)
    W = RWKV_WIDTH
    h = _rmsnorm(x_ref[...], g_ref[...]).astype(BF16)
    p = jnp.dot(h, win_ref[...], preferred_element_type=F32)

    ps = p[:, :D_SHIFT]
    rolled = pltpu.roll(ps, 1, 0)
    first = (i % tiles_per_seq) == 0
    prev = jnp.where(first, 0.0, prev_ref[...])
    sub = 8
    row = lax.broadcasted_iota(jnp.int32, (sub, 1), 0)
    shifted = jnp.concatenate([jnp.where(row == 0, prev, rolled[:sub]), rolled[sub:]], axis=0)
    prev_ref[...] = ps[tm - 1:tm, :]
    ps = ps + (shifted - ps) * mu_ref[...]

    r = ps[:, :W]
    k = ps[:, W:2 * W]
    v = ps[:, 2 * W:3 * W]
    slab = ps[:, 3 * W:3 * W + DECAY_LORA + AAA_LORA]
    pg = ps[:, 3 * W + DECAY_LORA + AAA_LORA:]
    lane = lax.broadcasted_iota(jnp.int32, (1, LANES), 1)
    slab = jnp.where(lane < DECAY_LORA, jnp.tanh(slab), slab)
    lora = _bdot(slab, lora_ref[...])
    lw_o[...] = -math.exp(-0.5) * _sigmoid(w0_ref[...] + lora[:, :W])
    a = _sigmoid(a0_ref[...] + lora[:, W:])
    g_o[...] = _bdot(_sigmoid(pg), gup_ref[...])

    lane_lo = lane < RWKV_HEAD
    kk = k * kk_ref[...]
    for hp in range(W // LANES):
        sl = slice(hp * LANES, (hp + 1) * LANES)
        kkp = kk[:, sl]
        ss = _head_sums(kkp * kkp, lane_lo)
        kkn = kkp * jnp.minimum(lax.rsqrt(ss), 1e12)
        na_o[:, sl] = -kkn
        bb_o[:, sl] = kkn * a[:, sl]
    r_o[...] = r
    k_o[...] = k * (1.0 + (a - 1.0) * ka_ref[...])
    v_o[...] = v

    u = _gelu(p[:, D_SHIFT:D_SHIFT + GMLP_WIDTH])
    vg = _gelu(p[:, D_SHIFT + GMLP_WIDTH:])
    mean = jnp.mean(vg, axis=-1, keepdims=True)
    d = vg - mean
    var = jnp.mean(d * d, axis=-1, keepdims=True)
    vn = (d * lax.rsqrt(var + LN_EPS) * lng_ref[...] + lnb_ref[...]).astype(BF16)
    tr = lax.broadcasted_iota(jnp.int32, (CHUNK, CHUNK), 0)
    tc = lax.broadcasted_iota(jnp.int32, (CHUNK, CHUNK), 1)
    ws = [jnp.where(tr >= tc, sw_ref[gi], 0.0).astype(BF16) for gi in range(GMLP_GROUPS)]
    bias = sb_ref[...]
    for c in range(tm // CHUNK):
        rows = slice(c * CHUNK, (c + 1) * CHUNK)
        outs = []
        for hp in range(GMLP_WIDTH // LANES):
            vp = vn[rows, hp * LANES:(hp + 1) * LANES]
            m0 = jnp.dot(ws[2 * hp], vp, preferred_element_type=F32)
            m1 = jnp.dot(ws[2 * hp + 1], vp, preferred_element_type=F32)
            outs.append(jnp.where(lane_lo, m0, m1))
        mixed = jnp.concatenate(outs, axis=1) + bias
        ygm_o[rows, :] = (u[rows, :] * mixed).astype(BF16)


def _mixin(x1, mix_g, w_in, mu, w0, a0, lora, g_up, k_k, k_a, ln_g, ln_b, sgu_w, sgu_bias, *, tm, seq):
    T, D = x1.shape
    W = RWKV_WIDTH
    row = lambda i: (i, 0)
    const = lambda i: (0, 0)
    full = lambda a: pl.BlockSpec(a.shape, (lambda i: (0,) * a.ndim), pipeline_mode=pl.Buffered(1))
    in_specs = [pl.BlockSpec((tm, D), row)] + [full(a) for a in
                (mix_g, w_in, mu, w0, a0, lora, g_up, k_k, k_a, ln_g, ln_b, sgu_w, sgu_bias)]
    out_shape = [jax.ShapeDtypeStruct((T, W), F32)] * 7 + [jax.ShapeDtypeStruct((T, GMLP_WIDTH), BF16)]
    out_specs = [pl.BlockSpec((tm, W), row)] * 7 + [pl.BlockSpec((tm, GMLP_WIDTH), row)]
    return pl.pallas_call(
        functools.partial(_mixin_kernel, tm=tm, tiles_per_seq=seq // tm),
        grid=(T // tm,),
        in_specs=in_specs,
        out_specs=out_specs,
        out_shape=out_shape,
        scratch_shapes=[pltpu.VMEM((1, D_SHIFT), F32)],
        compiler_params=pltpu.CompilerParams(
            dimension_semantics=("arbitrary",), vmem_limit_bytes=VMEM_LIMIT),
        name="mixin",
    )(x1, mix_g, w_in, mu, w0, a0, lora, g_up, k_k, k_a, ln_g, ln_b, sgu_w, sgu_bias)


def _dot_nn(a, b):
    return jnp.dot(a, b, preferred_element_type=F32)


def _dot_nt(a, b):
    return lax.dot_general(a, b, (((1,), (1,)), ((), ())), preferred_element_type=F32)


def _dot_tn(a, b):
    return lax.dot_general(a, b, (((0,), (0,)), ((), ())), preferred_element_type=F32)


def _cumsum_rows(tri, x):
    out = None
    for _ in range(3):
        hi = x.astype(BF16)
        t = _dot_nn(tri, hi)
        out = t if out is None else out + t
        x = x - hi.astype(F32)
    return out


def _scan_kernel(r_ref, k_ref, v_ref, na_ref, bb_ref, lw_ref, g_ref, gnw_ref, gnb_ref, rk_ref,
                 y_o, s_ref, yacc_ref, *, sb, nbat):
    C = SCAN_CHUNK
    n_pairs = RWKV_WIDTH // LANES
    sls = [slice(hp * LANES, (hp + 1) * LANES) for hp in range(n_pairs)]
    streams = [(b, hp) for b in range(nbat) for hp in range(n_pairs)]
    ids = range(len(streams))

    @pl.when(pl.program_id(1) == 0)
    def _():
        s_ref[...] = jnp.zeros_like(s_ref)

    lane = lax.broadcasted_iota(jnp.int32, (1, LANES), 1)
    lane_lo = lane < RWKV_HEAD
    ci = lax.broadcasted_iota(jnp.int32, (C, C), 0)
    cj = lax.broadcasted_iota(jnp.int32, (C, C), 1)
    tri = jnp.where(ci >= cj, 1.0, 0.0).astype(BF16)
    pt = lax.broadcasted_iota(jnp.int32, (C, LANES), 0)
    ps_ = lax.broadcasted_iota(jnp.int32, (C, LANES), 1) % C
    strict = ps_ < pt
    incl = ps_ <= pt
    eye = jnp.where(ps_ == pt, 1.0, 0.0).astype(F32)
    li = lax.broadcasted_iota(jnp.int32, (LANES, LANES), 0)
    lj = lax.broadcasted_iota(jnp.int32, (LANES, LANES), 1)
    same_head = (li // RWKV_HEAD) == (lj // RWKV_HEAD)

    def stack(x):
        return jnp.concatenate([jnp.where(lane_lo, x, 0.0), jnp.where(lane_lo, 0.0, x)], axis=0).astype(BF16)

    def prep(c):
        rows = slice(c * C, (c + 1) * C)
        x2, kb, kb_s, vs, d_end = [], [], [], [], []
        for b in range(nbat):
            lw = lw_ref[b, rows, :]
            cum = _cumsum_rows(tri, lw)
            e_pos = jnp.exp(cum)
            e_neg = jnp.exp(-cum)
            rt = r_ref[b, rows, :] * e_pos
            at = na_ref[b, rows, :] * jnp.exp(cum - lw)
            kh = k_ref[b, rows, :] * e_neg
            bh = bb_ref[b, rows, :] * e_neg
            vv = v_ref[b, rows, :]
            for sl in sls:
                x2.append(jnp.concatenate([at[:, sl], rt[:, sl]], axis=0).astype(BF16))
                kb.append(jnp.concatenate([kh[:, sl], bh[:, sl]], axis=0).astype(BF16))
                kb_s.append(jnp.concatenate([stack(kh[:, sl]), stack(bh[:, sl])], axis=0))
                vs.append((vv[:, sl].astype(BF16), stack(vv[:, sl])))
                d_end.append(e_pos[C - 1:C, sl])
        return x2, kb, kb_s, vs, d_end

    def head(ops, s):
        x2, kb, kb_s, vs, d_end = ops
        gram = [_dot_nt(x2[i], kb_s[i]) for i in ids]
        xs = [_dot_nt(x2[i], s[i].astype(BF16)) for i in ids]
        a_ab = [jnp.where(strict, gram[i][:C, LANES:], 0.0) for i in ids]
        a_kr = [jnp.concatenate([jnp.where(strict, gram[i][:C, :LANES], 0.0),
                                 jnp.where(incl, gram[i][C:, :LANES], 0.0)], axis=0).astype(BF16) for i in ids]
        a_rb = [jnp.where(incl, gram[i][C:, LANES:], 0.0).astype(BF16) for i in ids]
        av = [_dot_nn(a_kr[i], vs[i][1]) for i in ids]
        inv = [eye + a_ab[i] for i in ids]
        q = [_dot_nn(a_ab[i].astype(BF16), stack(a_ab[i])) for i in ids]
        for _ in range(int(math.log2(C)) - 2):
            t = [_dot_nn(jnp.concatenate([q[i], inv[i]], axis=0).astype(BF16), stack(q[i])) for i in ids]
            inv = [inv[i] + t[i][C:] for i in ids]
            q = [t[i][:C] for i in ids]
        inv = [inv[i] + _dot_nn(inv[i].astype(BF16), stack(q[i])) for i in ids]
        return inv, a_rb, xs, av

    def tail(c, ops, hd, s):
        x2, kb, kb_s, vs, d_end = ops
        inv, a_rb, xs, av = hd
        rows = slice(c * C, (c + 1) * C)
        u = [_dot_nn(inv[i].astype(BF16), stack(xs[i][:C] + av[i][:C])) for i in ids]
        yu = [_dot_nn(a_rb[i], stack(u[i])) for i in ids]
        upd = [_dot_tn(jnp.concatenate([vs[i][0], u[i].astype(BF16)], axis=0), kb[i])
               for i in ids]
        for i, (b, hp) in enumerate(streams):
            yacc_ref[b, rows, sls[hp]] = xs[i][C:] + av[i][C:] + yu[i]
        return [(s[i] + jnp.where(same_head, upd[i], 0.0)) * d_end[i] for i in ids]

    n_chunks = sb // C
    state = [s_ref[i] for i in ids]
    ops = prep(0)
    for c in range(n_chunks):
        hd = head(ops, state)
        nxt = prep(c + 1) if c + 1 < n_chunks else None
        state = tail(c, ops, hd, state)
        ops = nxt
    for i in ids:
        s_ref[i] = state[i]

    head_ones = jnp.where(same_head, 1.0, 0.0).astype(BF16)

    def head_sums2(x):
        hi = x.astype(BF16)
        lo = (x - hi.astype(F32)).astype(BF16)
        return _dot_nn(hi, head_ones) + _dot_nn(lo, head_ones)

    mean = [head_sums2(yacc_ref[b, :, sls[hp]]) * (1.0 / RWKV_HEAD) for b, hp in streams]
    rks = [_dot_nn((r_ref[b, :, sls[hp]] * k_ref[b, :, sls[hp]] * rk_ref[:, sls[hp]]).astype(BF16), head_ones)
           for b, hp in streams]
    d = [yacc_ref[b, :, sls[hp]] - mean[i] for i, (b, hp) in enumerate(streams)]
    var = [_dot_nn((d[i] * d[i]).astype(BF16), head_ones) * (1.0 / RWKV_HEAD) for i in ids]
    for i, (b, hp) in enumerate(streams):
        sl = sls[hp]
        yn = d[i] * lax.rsqrt(var[i] + GN_EPS) * gnw_ref[:, sl] + gnb_ref[:, sl]
        y_o[b, :, sl] = ((yn + rks[i] * v_ref[b, :, sl]) * g_ref[b, :, sl]).astype(BF16)


def _scan(r, k, v, na, bb, lw, g, gn_w, gn_b, r_k, *, batch, seq, sb, nbat):
    T, W = r.shape
    n_streams = nbat * (W // LANES)
    blk = lambda b, j: (b, j, 0)
    const = lambda b, j: (0, 0)
    in_specs = [pl.BlockSpec((nbat, sb, W), blk)] * 7 + [pl.BlockSpec((1, W), const)] * 3
    seq3 = lambda a: a.reshape(batch, seq, W)
    y = pl.pallas_call(
        functools.partial(_scan_kernel, sb=sb, nbat=nbat),
        grid=(batch // nbat, seq // sb),
        in_specs=in_specs,
        out_specs=pl.BlockSpec((nbat, sb, W), blk),
        out_shape=jax.ShapeDtypeStruct((batch, seq, W), BF16),
        scratch_shapes=[pltpu.VMEM((n_streams, LANES, LANES), F32), pltpu.VMEM((nbat, sb, W), F32)],
        compiler_params=pltpu.CompilerParams(
            dimension_semantics=("arbitrary", "arbitrary"), vmem_limit_bytes=VMEM_LIMIT),
        name="rwkv_scan",
    )(*(seq3(a) for a in (r, k, v, na, bb, lw, g)), gn_w, gn_b, r_k)
    return y.reshape(T, W)


def _tile(n, pref):
    return pref if n % pref == 0 else CHUNK


def kernel(x, ffn1_norm, ffn1_w1, ffn1_w3, ffn1_w2, mix_norm, w_in, mu_shift, w0, w_lora_up, a0, a_lora_up, g_lora_up, k_k, k_a, r_k, gn_w, gn_b, sgu_ln_g, sgu_ln_b, sgu_w, sgu_b, w_out, ffn2_norm, ffn2_w1, ffn2_w3, ffn2_w2, final_norm):
    B, S, D = x.shape
    T = B * S
    depth = ffn1_norm.shape[0]
    tm_ffn = _tile(S, 512)
    tm_mix = _tile(S, 512)
    sb = _tile(S, 256)
    rowv = lambda a: a.reshape(1, -1).astype(F32)

    xt = x.reshape(T, D)
    for l in range(depth):
        later = [w_in[l], w_out[l], ffn2_w1[l], ffn2_w3[l], ffn2_w2[l].reshape(D, -1)]
        sliceable = all(w.shape[0] % (T // tm_ffn) == 0 and (w.shape[0] // (T // tm_ffn)) % BF16_ROWS == 0
                        for w in later)
        f1, f3, f2 = _narrow([ffn1_w1[l], ffn1_w3[l], ffn1_w2[l].reshape(D, -1)], rows=NARROW_ROWS)
        xt, narrowed = _ffn(xt, rowv(ffn1_norm[l]), f1, f3, f2.reshape(ffn1_w2[l].shape), tm=tm_ffn,
                            narrow=later if sliceable else ())
        if not sliceable:
            narrowed = [w.astype(BF16) for w in later]
        w_in_b, w_out_b, w1_b, w3_b, w2_b = narrowed
        w2_b = w2_b.reshape(ffn2_w2[l].shape)
        zeros = jnp.zeros((DECAY_LORA, RWKV_WIDTH), F32)
        lora = jnp.concatenate([jnp.concatenate([w_lora_up[l], zeros], axis=1),
                                jnp.concatenate([zeros, a_lora_up[l]], axis=1)], axis=0)
        sgu_bias = jnp.repeat(jnp.transpose(sgu_b[l]), GMLP_WIDTH // GMLP_GROUPS, axis=1)
        r, k, v, na, bb, lw, g, y_gmlp = _mixin(
            xt, rowv(mix_norm[l]), w_in_b, rowv(mu_shift[l]), rowv(w0[l]), rowv(a0[l]),
            lora.astype(BF16), g_lora_up[l].astype(BF16), rowv(k_k[l]), rowv(k_a[l]),
            rowv(sgu_ln_g[l]), rowv(sgu_ln_b[l]), sgu_w[l], sgu_bias, tm=tm_mix, seq=S)
        y_rwkv = _scan(r, k, v, na, bb, lw, g, rowv(gn_w[l]), rowv(gn_b[l]), rowv(r_k[l]),
                       batch=B, seq=S, sb=sb, nbat=SCAN_BATCH if B % SCAN_BATCH == 0 else 1)
        last = l == depth - 1
        xt, _ = _ffn(xt, rowv(ffn2_norm[l]), w1_b, w3_b, w2_b, tm=tm_ffn,
                     mix_in=(y_rwkv, y_gmlp, w_out_b),
                     final_g=rowv(final_norm) if last else None)
    if depth == 0:
        raise ValueError("depth must be positive")
    return xt.reshape(B, S, D)
```

```python
import functools
import math

import jax
import jax.numpy as jnp
from jax import lax
from jax.experimental import pallas as pl
from jax.experimental.pallas import tpu as pltpu

F32 = jnp.float32
BF16 = jnp.bfloat16

RWKV_WIDTH = 512
RWKV_HEAD = 64
GMLP_WIDTH = 512
GMLP_GROUPS = 8
CHUNK = 128
DECAY_LORA = 64
AAA_LORA = 64
GATE_LORA = 128
D_SHIFT = 3 * RWKV_WIDTH + DECAY_LORA + AAA_LORA + GATE_LORA
RMS_EPS = 1e-6
GN_EPS = 64e-5
LN_EPS = 1e-5
FFN_RES_SCALE = 0.5

LANES = 128
BF16_ROWS = 16
NARROW_STEPS = 8
SCAN_CHUNK = 64
SCAN_BATCH = 4
VMEM_LIMIT = 56 * 1024 * 1024


def _rmsnorm(x, g):
    return x * lax.rsqrt(jnp.mean(x * x, axis=-1, keepdims=True) + RMS_EPS) * g


def _gelu(x):
    return 0.5 * x * (1.0 + lax.erf(x * (1.0 / math.sqrt(2.0))))


def _sigmoid(x):
    return 1.0 / (1.0 + jnp.exp(-x))


def _bdot(a, b):
    return jnp.dot(a.astype(BF16), b.astype(BF16), preferred_element_type=F32)


def _ffn_kernel(*refs, mix, final, n_cast):
    it = iter(refs)
    x_ref = next(it)
    if mix:
        yr_ref, yg_ref, wo_ref = next(it), next(it), next(it)
    g_ref, w1_ref, w3_ref, w2_ref = next(it), next(it), next(it), next(it)
    fg_ref = next(it) if final else None
    cast_in = [next(it) for _ in range(n_cast)]
    o_ref = next(it)
    cast_out = [next(it) for _ in range(n_cast)]

    x = x_ref[...]
    if mix:
        x = x + jnp.dot(yr_ref[...], wo_ref[:RWKV_WIDTH, :], preferred_element_type=F32)
        x = x + jnp.dot(yg_ref[...], wo_ref[RWKV_WIDTH:, :], preferred_element_type=F32)
    h = _rmsnorm(x, g_ref[...]).astype(BF16)
    a = jnp.dot(h, w1_ref[...], preferred_element_type=F32)
    b = jnp.dot(h, w3_ref[...], preferred_element_type=F32)
    z = (a * _sigmoid(a) * b).astype(BF16)
    y = x + FFN_RES_SCALE * jnp.dot(z, w2_ref[...], preferred_element_type=F32)
    if final:
        y = _rmsnorm(y, fg_ref[...])
    o_ref[...] = y
    for src, dst in zip(cast_in, cast_out):
        dst[...] = src[...].astype(BF16)


def _narrow_kernel(*refs):
    half = len(refs) // 2
    for src, dst in zip(refs[:half], refs[half:]):
        dst[...] = src[...].astype(BF16)


def _row_blocks(n_rows, n_steps):
    k = 1
    while k <= n_steps:
        if (n_rows * k) % n_steps == 0 and n_steps % k == 0:
            rows = n_rows * k // n_steps
            if rows % BF16_ROWS == 0:
                return rows, k
        k *= 2
    return None


def _narrow_specs(arrs, layer, n_steps):
    specs = []
    for a in arrs:
        rows, k = _row_blocks(a.shape[1], n_steps)
        specs.append(pl.BlockSpec((None, rows, a.shape[2]), lambda i, k=k: (layer, i // k, 0)))
    return specs


def _narrow(arrs, layer, *, n_steps):
    return pl.pallas_call(
        _narrow_kernel,
        grid=(n_steps,),
        in_specs=_narrow_specs(arrs, layer, n_steps),
        out_specs=_narrow_specs(arrs, 0, n_steps),
        out_shape=[jax.ShapeDtypeStruct((1,) + a.shape[1:], BF16) for a in arrs],
        compiler_params=pltpu.CompilerParams(dimension_semantics=("arbitrary",), vmem_limit_bytes=VMEM_LIMIT),
        name="narrow_weights",
    )(*arrs)


def _whole(a, layered=False):
    if layered:
        return pl.BlockSpec((None,) + a.shape[1:], lambda i: (0, 0, 0), pipeline_mode=pl.Buffered(1))
    return pl.BlockSpec(a.shape, lambda i: (0,) * a.ndim, pipeline_mode=pl.Buffered(1))


def _ffn(x, norm_g, w1, w3, w2, *, tm, mix_in=None, final_g=None, narrow=(), layer=0):
    T, D = x.shape
    n = T // tm
    mix = mix_in is not None
    final = final_g is not None
    row = lambda i: (i, 0)
    whole = lambda a: _whole(a, layered=a.ndim == 3)
    in_specs = [pl.BlockSpec((tm, D), row)]
    args = [x]
    if mix:
        yr, yg, wo = mix_in
        in_specs += [pl.BlockSpec((tm, yr.shape[1]), row), pl.BlockSpec((tm, yg.shape[1]), row), whole(wo)]
        args += [yr, yg, wo]
    in_specs += [whole(norm_g), whole(w1), whole(w3), whole(w2)]
    args += [norm_g, w1, w3, w2]
    if final:
        in_specs.append(whole(final_g))
        args.append(final_g)
    out_specs = [pl.BlockSpec((tm, D), row)]
    out_shape = [jax.ShapeDtypeStruct((T, D), F32)]
    in_specs += _narrow_specs(narrow, layer, n)
    args += list(narrow)
    out_specs += _narrow_specs(narrow, 0, n)
    out_shape += [jax.ShapeDtypeStruct((1,) + a.shape[1:], BF16) for a in narrow]
    outs = pl.pallas_call(
        functools.partial(_ffn_kernel, mix=mix, final=final, n_cast=len(narrow)),
        grid=(n,),
        in_specs=in_specs,
        out_specs=out_specs,
        out_shape=out_shape,
        compiler_params=pltpu.CompilerParams(
            dimension_semantics=("arbitrary",), vmem_limit_bytes=VMEM_LIMIT),
        name="ffn_mix_final" if mix else "ffn",
    )(*args)
    return outs[0], list(outs[1:])


def _head_sums(x, lane_lo):
    s0 = jnp.sum(jnp.where(lane_lo, x, 0.0), axis=-1, keepdims=True)
    s1 = jnp.sum(jnp.where(lane_lo, 0.0, x), axis=-1, keepdims=True)
    return jnp.where(lane_lo, s0, s1)


def _mixin_kernel(x_ref, g_ref, win_ref, mu_ref, w0_ref, a0_ref, lora_ref, gup_ref, kk_ref, ka_ref,
                  lng_ref, lnb_ref, sw_ref, sb_ref,
                  r_o, k_o, v_o, na_o, bb_o, lw_o, g_o, ygm_o, prev_ref, *, tm, tiles_per_seq):
    i = pl.program_id(0)
    W = RWKV_WIDTH
    h = _rmsnorm(x_ref[...], g_ref[...]).astype(BF16)
    p = jnp.dot(h, win_ref[...], preferred_element_type=F32)

    ps = p[:, :D_SHIFT]
    rolled = pltpu.roll(ps, 1, 0)
    first = (i % tiles_per_seq) == 0
    prev = jnp.where(first, 0.0, prev_ref[...])
    sub = 8
    row = lax.broadcasted_iota(jnp.int32, (sub, 1), 0)
    shifted = jnp.concatenate([jnp.where(row == 0, prev, rolled[:sub]), rolled[sub:]], axis=0)
    prev_ref[...] = ps[tm - 1:tm, :]
    ps = ps + (shifted - ps) * mu_ref[...]

    r = ps[:, :W]
    k = ps[:, W:2 * W]
    v = ps[:, 2 * W:3 * W]
    slab = ps[:, 3 * W:3 * W + DECAY_LORA + AAA_LORA]
    pg = ps[:, 3 * W + DECAY_LORA + AAA_LORA:]
    lane = lax.broadcasted_iota(jnp.int32, (1, LANES), 1)
    slab = jnp.where(lane < DECAY_LORA, jnp.tanh(slab), slab)
    lora = _bdot(slab, lora_ref[...])
    lw_o[...] = -math.exp(-0.5) * _sigmoid(w0_ref[...] + lora[:, :W])
    a = _sigmoid(a0_ref[...] + lora[:, W:])
    g_o[...] = _bdot(_sigmoid(pg), gup_ref[...])

    lane_lo = lane < RWKV_HEAD
    kk = k * kk_ref[...]
    for hp in range(W // LANES):
        sl = slice(hp * LANES, (hp + 1) * LANES)
        kkp = kk[:, sl]
        ss = _head_sums(kkp * kkp, lane_lo)
        kkn = kkp * jnp.minimum(lax.rsqrt(ss), 1e12)
        na_o[:, sl] = -kkn
        bb_o[:, sl] = kkn * a[:, sl]
    r_o[...] = r
    k_o[...] = k * (1.0 + (a - 1.0) * ka_ref[...])
    v_o[...] = v

    u = _gelu(p[:, D_SHIFT:D_SHIFT + GMLP_WIDTH])
    vg = _gelu(p[:, D_SHIFT + GMLP_WIDTH:])
    mean = jnp.mean(vg, axis=-1, keepdims=True)
    d = vg - mean
    var = jnp.mean(d * d, axis=-1, keepdims=True)
    vn = (d * lax.rsqrt(var + LN_EPS) * lng_ref[...] + lnb_ref[...]).astype(BF16)
    tr = lax.broadcasted_iota(jnp.int32, (CHUNK, CHUNK), 0)
    tc = lax.broadcasted_iota(jnp.int32, (CHUNK, CHUNK), 1)
    ws = [jnp.where(tr >= tc, sw_ref[gi], 0.0).astype(BF16) for gi in range(GMLP_GROUPS)]
    bias = sb_ref[...]
    for c in range(tm // CHUNK):
        rows = slice(c * CHUNK, (c + 1) * CHUNK)
        outs = []
        for hp in range(GMLP_WIDTH // LANES):
            vp = vn[rows, hp * LANES:(hp + 1) * LANES]
            m0 = jnp.dot(ws[2 * hp], vp, preferred_element_type=F32)
            m1 = jnp.dot(ws[2 * hp + 1], vp, preferred_element_type=F32)
            outs.append(jnp.where(lane_lo, m0, m1))
        mixed = jnp.concatenate(outs, axis=1) + bias
        ygm_o[rows, :] = (u[rows, :] * mixed).astype(BF16)


def _mixin(x1, mix_g, w_in, mu, w0, a0, lora, g_up, k_k, k_a, ln_g, ln_b, sgu_w, sgu_bias, *, tm, seq):
    T, D = x1.shape
    W = RWKV_WIDTH
    row = lambda i: (i, 0)
    const = lambda i: (0, 0)
    in_specs = [pl.BlockSpec((tm, D), row)] + [_whole(a, layered=a is w_in) for a in
                (mix_g, w_in, mu, w0, a0, lora, g_up, k_k, k_a, ln_g, ln_b, sgu_w, sgu_bias)]
    out_shape = [jax.ShapeDtypeStruct((T, W), F32)] * 7 + [jax.ShapeDtypeStruct((T, GMLP_WIDTH), BF16)]
    out_specs = [pl.BlockSpec((tm, W), row)] * 7 + [pl.BlockSpec((tm, GMLP_WIDTH), row)]
    return pl.pallas_call(
        functools.partial(_mixin_kernel, tm=tm, tiles_per_seq=seq // tm),
        grid=(T // tm,),
        in_specs=in_specs,
        out_specs=out_specs,
        out_shape=out_shape,
        scratch_shapes=[pltpu.VMEM((1, D_SHIFT), F32)],
        compiler_params=pltpu.CompilerParams(
            dimension_semantics=("arbitrary",), vmem_limit_bytes=VMEM_LIMIT),
        name="mixin",
    )(x1, mix_g, w_in, mu, w0, a0, lora, g_up, k_k, k_a, ln_g, ln_b, sgu_w, sgu_bias)


def _dot_nn(a, b):
    return jnp.dot(a, b, preferred_element_type=F32)


def _dot_nt(a, b):
    return lax.dot_general(a, b, (((1,), (1,)), ((), ())), preferred_element_type=F32)


def _dot_tn(a, b):
    return lax.dot_general(a, b, (((0,), (0,)), ((), ())), preferred_element_type=F32)


def _cumsum_rows(tri, x):
    out = None
    for _ in range(3):
        hi = x.astype(BF16)
        t = _dot_nn(tri, hi)
        out = t if out is None else out + t
        x = x - hi.astype(F32)
    return out


def _scan_kernel(r_ref, k_ref, v_ref, na_ref, bb_ref, lw_ref, g_ref, gnw_ref, gnb_ref, rk_ref,
                 y_o, s_ref, yacc_ref, *, sb, nbat):
    C = SCAN_CHUNK
    n_pairs = RWKV_WIDTH // LANES
    sls = [slice(hp * LANES, (hp + 1) * LANES) for hp in range(n_pairs)]
    streams = [(b, hp) for b in range(nbat) for hp in range(n_pairs)]
    ids = range(len(streams))

    @pl.when(pl.program_id(1) == 0)
    def _():
        s_ref[...] = jnp.zeros_like(s_ref)

    lane = lax.broadcasted_iota(jnp.int32, (1, LANES), 1)
    lane_lo = lane < RWKV_HEAD
    ci = lax.broadcasted_iota(jnp.int32, (C, C), 0)
    cj = lax.broadcasted_iota(jnp.int32, (C, C), 1)
    tri = jnp.where(ci >= cj, 1.0, 0.0).astype(BF16)
    pt = lax.broadcasted_iota(jnp.int32, (C, LANES), 0)
    ps_ = lax.broadcasted_iota(jnp.int32, (C, LANES), 1) % C
    strict = ps_ < pt
    incl = ps_ <= pt
    eye = jnp.where(ps_ == pt, 1.0, 0.0).astype(F32)
    li = lax.broadcasted_iota(jnp.int32, (LANES, LANES), 0)
    lj = lax.broadcasted_iota(jnp.int32, (LANES, LANES), 1)
    same_head = (li // RWKV_HEAD) == (lj // RWKV_HEAD)

    def stack(x):
        return jnp.concatenate([jnp.where(lane_lo, x, 0.0), jnp.where(lane_lo, 0.0, x)], axis=0).astype(BF16)

    def prep(c):
        rows = slice(c * C, (c + 1) * C)
        x2, kb, kb_s, vs, d_end = [], [], [], [], []
        for b in range(nbat):
            lw = lw_ref[b, rows, :]
            cum = _cumsum_rows(tri, lw)
            e_pos = jnp.exp(cum)
            e_neg = jnp.exp(-cum)
            rt = r_ref[b, rows, :] * e_pos
            at = na_ref[b, rows, :] * jnp.exp(cum - lw)
            kh = k_ref[b, rows, :] * e_neg
            bh = bb_ref[b, rows, :] * e_neg
            vv = v_ref[b, rows, :]
            for sl in sls:
                x2.append(jnp.concatenate([at[:, sl], rt[:, sl]], axis=0).astype(BF16))
                kb.append(jnp.concatenate([kh[:, sl], bh[:, sl]], axis=0).astype(BF16))
                kb_s.append(jnp.concatenate([stack(kh[:, sl]), stack(bh[:, sl])], axis=0))
                vs.append((vv[:, sl].astype(BF16), stack(vv[:, sl])))
                d_end.append(e_pos[C - 1:C, sl])
        return x2, kb, kb_s, vs, d_end

    def head(ops, s):
        x2, kb, kb_s, vs, d_end = ops
        gram = [_dot_nt(x2[i], kb_s[i]) for i in ids]
        xs = [_dot_nt(x2[i], s[i].astype(BF16)) for i in ids]
        a_ab = [jnp.where(strict, gram[i][:C, LANES:], 0.0) for i in ids]
        a_kr = [jnp.concatenate([jnp.where(strict, gram[i][:C, :LANES], 0.0),
                                 jnp.where(incl, gram[i][C:, :LANES], 0.0)], axis=0).astype(BF16) for i in ids]
        a_rb = [jnp.where(incl, gram[i][C:, LANES:], 0.0).astype(BF16) for i in ids]
        av = [_dot_nn(a_kr[i], vs[i][1]) for i in ids]
        inv = [eye + a_ab[i] for i in ids]
        q = [_dot_nn(a_ab[i].astype(BF16), stack(a_ab[i])) for i in ids]
        for _ in range(int(math.log2(C)) - 2):
            t = [_dot_nn(jnp.concatenate([q[i], inv[i]], axis=0).astype(BF16), stack(q[i])) for i in ids]
            inv = [inv[i] + t[i][C:] for i in ids]
            q = [t[i][:C] for i in ids]
        inv = [inv[i] + _dot_nn(inv[i].astype(BF16), stack(q[i])) for i in ids]
        return inv, a_rb, xs, av

    def tail(c, ops, hd, s):
        x2, kb, kb_s, vs, d_end = ops
        inv, a_rb, xs, av = hd
        rows = slice(c * C, (c + 1) * C)
        u = [_dot_nn(inv[i].astype(BF16), stack(xs[i][:C] + av[i][:C])) for i in ids]
        yu = [_dot_nn(a_rb[i], stack(u[i])) for i in ids]
        upd = [_dot_tn(jnp.concatenate([vs[i][0], u[i].astype(BF16)], axis=0), kb[i])
               for i in ids]
        for i, (b, hp) in enumerate(streams):
            yacc_ref[b, rows, sls[hp]] = xs[i][C:] + av[i][C:] + yu[i]
        return [(s[i] + jnp.where(same_head, upd[i], 0.0)) * d_end[i] for i in ids]

    n_chunks = sb // C
    state = [s_ref[i] for i in ids]
    ops = prep(0)
    for c in range(n_chunks):
        hd = head(ops, state)
        nxt = prep(c + 1) if c + 1 < n_chunks else None
        state = tail(c, ops, hd, state)
        ops = nxt
    for i in ids:
        s_ref[i] = state[i]

    head_ones = jnp.where(same_head, 1.0, 0.0).astype(BF16)

    def head_sums2(x):
        hi = x.astype(BF16)
        lo = (x - hi.astype(F32)).astype(BF16)
        return _dot_nn(hi, head_ones) + _dot_nn(lo, head_ones)

    mean = [head_sums2(yacc_ref[b, :, sls[hp]]) * (1.0 / RWKV_HEAD) for b, hp in streams]
    rks = [_dot_nn((r_ref[b, :, sls[hp]] * k_ref[b, :, sls[hp]] * rk_ref[:, sls[hp]]).astype(BF16), head_ones)
           for b, hp in streams]
    d = [yacc_ref[b, :, sls[hp]] - mean[i] for i, (b, hp) in enumerate(streams)]
    var = [_dot_nn((d[i] * d[i]).astype(BF16), head_ones) * (1.0 / RWKV_HEAD) for i in ids]
    for i, (b, hp) in enumerate(streams):
        sl = sls[hp]
        yn = d[i] * lax.rsqrt(var[i] + GN_EPS) * gnw_ref[:, sl] + gnb_ref[:, sl]
        y_o[b, :, sl] = ((yn + rks[i] * v_ref[b, :, sl]) * g_ref[b, :, sl]).astype(BF16)


def _scan(r, k, v, na, bb, lw, g, gn_w, gn_b, r_k, *, batch, seq, sb, nbat):
    T, W = r.shape
    n_streams = nbat * (W // LANES)
    blk = lambda b, j: (b, j, 0)
    const = lambda b, j: (0, 0)
    in_specs = [pl.BlockSpec((nbat, sb, W), blk)] * 7 + [pl.BlockSpec((1, W), const)] * 3
    seq3 = lambda a: a.reshape(batch, seq, W)
    y = pl.pallas_call(
        functools.partial(_scan_kernel, sb=sb, nbat=nbat),
        grid=(batch // nbat, seq // sb),
        in_specs=in_specs,
        out_specs=pl.BlockSpec((nbat, sb, W), blk),
        out_shape=jax.ShapeDtypeStruct((batch, seq, W), BF16),
        scratch_shapes=[pltpu.VMEM((n_streams, LANES, LANES), F32), pltpu.VMEM((nbat, sb, W), F32)],
        compiler_params=pltpu.CompilerParams(
            dimension_semantics=("arbitrary", "arbitrary"), vmem_limit_bytes=VMEM_LIMIT),
        name="rwkv_scan",
    )(*(seq3(a) for a in (r, k, v, na, bb, lw, g)), gn_w, gn_b, r_k)
    return y.reshape(T, W)


def _tile(n, pref):
    return pref if n % pref == 0 else CHUNK


def kernel(x, ffn1_norm, ffn1_w1, ffn1_w3, ffn1_w2, mix_norm, w_in, mu_shift, w0, w_lora_up, a0, a_lora_up, g_lora_up, k_k, k_a, r_k, gn_w, gn_b, sgu_ln_g, sgu_ln_b, sgu_w, sgu_b, w_out, ffn2_norm, ffn2_w1, ffn2_w3, ffn2_w2, final_norm):
    B, S, D = x.shape
    T = B * S
    depth = ffn1_norm.shape[0]
    tm_ffn = _tile(S, 512)
    tm_mix = _tile(S, 512)
    sb = _tile(S, 256)
    rowv = lambda a: a.reshape(1, -1).astype(F32)

    xt = x.reshape(T, D)
    for l in range(depth):
        f1, f3, f2 = _narrow([ffn1_w1, ffn1_w3, ffn1_w2], l, n_steps=NARROW_STEPS)
        later = [w_in, w_out, ffn2_w1, ffn2_w3, ffn2_w2]
        xt, (w_in_b, w_out_b, w1_b, w3_b, w2_b) = _ffn(
            xt, rowv(ffn1_norm[l]), f1, f3, f2, tm=tm_ffn, narrow=later, layer=l)
        zeros = jnp.zeros((DECAY_LORA, RWKV_WIDTH), F32)
        lora = jnp.concatenate([jnp.concatenate([w_lora_up[l], zeros], axis=1),
                                jnp.concatenate([zeros, a_lora_up[l]], axis=1)], axis=0)
        sgu_bias = jnp.repeat(jnp.transpose(sgu_b[l]), GMLP_WIDTH // GMLP_GROUPS, axis=1)
        r, k, v, na, bb, lw, g, y_gmlp = _mixin(
            xt, rowv(mix_norm[l]), w_in_b, rowv(mu_shift[l]), rowv(w0[l]), rowv(a0[l]),
            lora.astype(BF16), g_lora_up[l].astype(BF16), rowv(k_k[l]), rowv(k_a[l]),
            rowv(sgu_ln_g[l]), rowv(sgu_ln_b[l]), sgu_w[l], sgu_bias, tm=tm_mix, seq=S)
        y_rwkv = _scan(r, k, v, na, bb, lw, g, rowv(gn_w[l]), rowv(gn_b[l]), rowv(r_k[l]),
                       batch=B, seq=S, sb=sb, nbat=SCAN_BATCH if B % SCAN_BATCH == 0 else 1)
        last = l == depth - 1
        xt, _ = _ffn(xt, rowv(ffn2_norm[l]), w1_b, w3_b, w2_b, tm=tm_ffn,
                     mix_in=(y_rwkv, y_gmlp, w_out_b),
                     final_g=rowv(final_norm) if last else None)
    if depth == 0:
        raise ValueError("depth must be positive")
    return xt.reshape(B, S, D)
```

```python
import functools
import math

import jax
import jax.numpy as jnp
from jax import lax
from jax.experimental import pallas as pl
from jax.experimental.pallas import tpu as pltpu

F32 = jnp.float32
BF16 = jnp.bfloat16

RWKV_WIDTH = 512
RWKV_HEAD = 64
GMLP_WIDTH = 512
GMLP_GROUPS = 8
CHUNK = 128
DECAY_LORA = 64
AAA_LORA = 64
GATE_LORA = 128
D_SHIFT = 3 * RWKV_WIDTH + DECAY_LORA + AAA_LORA + GATE_LORA
RMS_EPS = 1e-6
GN_EPS = 64e-5
LN_EPS = 1e-5
FFN_RES_SCALE = 0.5

LANES = 128
BF16_ROWS = 16
NARROW_STEPS = 8
SCAN_CHUNK = 64
SCAN_BATCH = 4
SUBLANES = 8
V7X_VMEM_BYTES = 64 * 1024 * 1024
VMEM_LIMIT = V7X_VMEM_BYTES * 7 // 8


def _rmsnorm(x, g):
    return x * lax.rsqrt(jnp.mean(x * x, axis=-1, keepdims=True) + RMS_EPS) * g


def _gelu(x):
    return 0.5 * x * (1.0 + lax.erf(x * (1.0 / math.sqrt(2.0))))


def _sigmoid(x):
    return 1.0 / (1.0 + jnp.exp(-x))


def _bdot(a, b):
    return jnp.dot(a.astype(BF16), b.astype(BF16), preferred_element_type=F32)


def _ffn_kernel(*refs, mix, final, n_cast):
    it = iter(refs)
    x_ref = next(it)
    if mix:
        yr_ref, yg_ref, wo_ref = next(it), next(it), next(it)
    g_ref, w1_ref, w3_ref, w2_ref = next(it), next(it), next(it), next(it)
    fg_ref = next(it) if final else None
    cast_in = [next(it) for _ in range(n_cast)]
    o_ref = next(it)
    cast_out = [next(it) for _ in range(n_cast)]

    x = x_ref[...]
    if mix:
        x = x + jnp.dot(yr_ref[...], wo_ref[:RWKV_WIDTH, :], preferred_element_type=F32)
        x = x + jnp.dot(yg_ref[...], wo_ref[RWKV_WIDTH:, :], preferred_element_type=F32)
    h = _rmsnorm(x, g_ref[...]).astype(BF16)
    a = jnp.dot(h, w1_ref[...], preferred_element_type=F32)
    b = jnp.dot(h, w3_ref[...], preferred_element_type=F32)
    z = (a * _sigmoid(a) * b).astype(BF16)
    y = x + FFN_RES_SCALE * jnp.dot(z, w2_ref[...], preferred_element_type=F32)
    if final:
        y = _rmsnorm(y, fg_ref[...])
    o_ref[...] = y
    for src, dst in zip(cast_in, cast_out):
        dst[...] = src[...].astype(BF16)


def _narrow_kernel(*refs):
    half = len(refs) // 2
    for src, dst in zip(refs[:half], refs[half:]):
        dst[...] = src[...].astype(BF16)


def _row_blocks(n_rows, n_steps):
    k = 1
    while k <= n_steps:
        if (n_rows * k) % n_steps == 0 and n_steps % k == 0:
            rows = n_rows * k // n_steps
            if rows % BF16_ROWS == 0:
                return rows, k
        k *= 2
    return None


def _narrow_specs(arrs, layer, n_steps):
    specs = []
    for a in arrs:
        rows, k = _row_blocks(a.shape[1], n_steps)
        specs.append(pl.BlockSpec((None, rows, a.shape[2]), lambda i, k=k: (layer, i // k, 0)))
    return specs


def _narrow(arrs, layer, *, n_steps):
    return pl.pallas_call(
        _narrow_kernel,
        grid=(n_steps,),
        in_specs=_narrow_specs(arrs, layer, n_steps),
        out_specs=_narrow_specs(arrs, 0, n_steps),
        out_shape=[jax.ShapeDtypeStruct((1,) + a.shape[1:], BF16) for a in arrs],
        compiler_params=pltpu.CompilerParams(dimension_semantics=("arbitrary",), vmem_limit_bytes=VMEM_LIMIT),
        name="narrow_weights",
    )(*arrs)


def _whole(a, layered=False):
    if layered:
        return pl.BlockSpec((None,) + a.shape[1:], lambda i: (0, 0, 0), pipeline_mode=pl.Buffered(1))
    return pl.BlockSpec(a.shape, lambda i: (0,) * a.ndim, pipeline_mode=pl.Buffered(1))


def _ffn(x, norm_g, w1, w3, w2, *, tm, mix_in=None, final_g=None, narrow=(), layer=0):
    T, D = x.shape
    n = T // tm
    mix = mix_in is not None
    final = final_g is not None
    row = lambda i: (i, 0)
    whole = lambda a: _whole(a, layered=a.ndim == 3)
    in_specs = [pl.BlockSpec((tm, D), row)]
    args = [x]
    if mix:
        yr, yg, wo = mix_in
        in_specs += [pl.BlockSpec((tm, yr.shape[1]), row), pl.BlockSpec((tm, yg.shape[1]), row), whole(wo)]
        args += [yr, yg, wo]
    in_specs += [whole(norm_g), whole(w1), whole(w3), whole(w2)]
    args += [norm_g, w1, w3, w2]
    if final:
        in_specs.append(whole(final_g))
        args.append(final_g)
    out_specs = [pl.BlockSpec((tm, D), row)]
    out_shape = [jax.ShapeDtypeStruct((T, D), F32)]
    in_specs += _narrow_specs(narrow, layer, n)
    args += list(narrow)
    out_specs += _narrow_specs(narrow, 0, n)
    out_shape += [jax.ShapeDtypeStruct((1,) + a.shape[1:], BF16) for a in narrow]
    outs = pl.pallas_call(
        functools.partial(_ffn_kernel, mix=mix, final=final, n_cast=len(narrow)),
        grid=(n,),
        in_specs=in_specs,
        out_specs=out_specs,
        out_shape=out_shape,
        compiler_params=pltpu.CompilerParams(
            dimension_semantics=("arbitrary",), vmem_limit_bytes=VMEM_LIMIT),
        name="ffn_mix_final" if mix else "ffn",
    )(*args)
    return outs[0], list(outs[1:])


def _head_sums(x, lane_lo):
    s0 = jnp.sum(jnp.where(lane_lo, x, 0.0), axis=-1, keepdims=True)
    s1 = jnp.sum(jnp.where(lane_lo, 0.0, x), axis=-1, keepdims=True)
    return jnp.where(lane_lo, s0, s1)


def _mixin_kernel(x_ref, g_ref, win_ref, mu_ref, w0_ref, a0_ref, lora_ref, gup_ref, kk_ref, ka_ref,
                  lng_ref, lnb_ref, sw_ref, sb_ref,
                  r_o, k_o, v_o, na_o, bb_o, lw_o, g_o, ygm_o, prev_ref, *, tm, tiles_per_seq):
    i = pl.program_id(0)
    W = RWKV_WIDTH
    h = _rmsnorm(x_ref[...], g_ref[...]).astype(BF16)
    p = jnp.dot(h, win_ref[...], preferred_element_type=F32)

    ps = p[:, :D_SHIFT]
    rolled = pltpu.roll(ps, 1, 0)
    first = (i % tiles_per_seq) == 0
    prev = jnp.where(first, 0.0, prev_ref[...])
    row = lax.broadcasted_iota(jnp.int32, (SUBLANES, 1), 0)
    shifted = jnp.concatenate([jnp.where(row == 0, prev, rolled[:SUBLANES]), rolled[SUBLANES:]], axis=0)
    prev_ref[...] = ps[tm - 1:tm, :]
    ps = ps + (shifted - ps) * mu_ref[...]

    r = ps[:, :W]
    k = ps[:, W:2 * W]
    v = ps[:, 2 * W:3 * W]
    slab = ps[:, 3 * W:3 * W + DECAY_LORA + AAA_LORA]
    pg = ps[:, 3 * W + DECAY_LORA + AAA_LORA:]
    lane = lax.broadcasted_iota(jnp.int32, (1, LANES), 1)
    slab = jnp.where(lane < DECAY_LORA, jnp.tanh(slab), slab)
    lora = _bdot(slab, lora_ref[...])
    lw_o[...] = -math.exp(-0.5) * _sigmoid(w0_ref[...] + lora[:, :W])
    a = _sigmoid(a0_ref[...] + lora[:, W:])
    g_o[...] = _bdot(_sigmoid(pg), gup_ref[...]).astype(BF16)

    lane_lo = lane < RWKV_HEAD
    kk = k * kk_ref[...]
    for hp in range(W // LANES):
        sl = slice(hp * LANES, (hp + 1) * LANES)
        kkp = kk[:, sl]
        ss = _head_sums(kkp * kkp, lane_lo)
        kkn = kkp * jnp.minimum(lax.rsqrt(ss), 1e12)
        na_o[:, sl] = (-kkn).astype(BF16)
        bb_o[:, sl] = (kkn * a[:, sl]).astype(BF16)
    r_o[...] = r.astype(BF16)
    k_o[...] = (k * (1.0 + (a - 1.0) * ka_ref[...])).astype(BF16)
    v_o[...] = v.astype(BF16)

    u = _gelu(p[:, D_SHIFT:D_SHIFT + GMLP_WIDTH])
    vg = _gelu(p[:, D_SHIFT + GMLP_WIDTH:])
    mean = jnp.mean(vg, axis=-1, keepdims=True)
    d = vg - mean
    var = jnp.mean(d * d, axis=-1, keepdims=True)
    vn = (d * lax.rsqrt(var + LN_EPS) * lng_ref[...] + lnb_ref[...]).astype(BF16)
    tr = lax.broadcasted_iota(jnp.int32, (CHUNK, CHUNK), 0)
    tc = lax.broadcasted_iota(jnp.int32, (CHUNK, CHUNK), 1)
    ws = [jnp.where(tr >= tc, sw_ref[gi], 0.0).astype(BF16) for gi in range(GMLP_GROUPS)]
    bias = sb_ref[...]
    for c in range(tm // CHUNK):
        rows = slice(c * CHUNK, (c + 1) * CHUNK)
        outs = []
        for hp in range(GMLP_WIDTH // LANES):
            vp = vn[rows, hp * LANES:(hp + 1) * LANES]
            m0 = jnp.dot(ws[2 * hp], vp, preferred_element_type=F32)
            m1 = jnp.dot(ws[2 * hp + 1], vp, preferred_element_type=F32)
            outs.append(jnp.where(lane_lo, m0, m1))
        mixed = jnp.concatenate(outs, axis=1) + bias
        ygm_o[rows, :] = (u[rows, :] * mixed).astype(BF16)


def _mixin(x1, mix_g, w_in, mu, w0, a0, lora, g_up, k_k, k_a, ln_g, ln_b, sgu_w, sgu_bias, *, tm, seq):
    T, D = x1.shape
    W = RWKV_WIDTH
    row = lambda i: (i, 0)
    const = lambda i: (0, 0)
    in_specs = [pl.BlockSpec((tm, D), row)] + [_whole(a, layered=a is w_in) for a in
                (mix_g, w_in, mu, w0, a0, lora, g_up, k_k, k_a, ln_g, ln_b, sgu_w, sgu_bias)]
    dts = [BF16, BF16, BF16, BF16, BF16, F32, BF16]
    out_shape = [jax.ShapeDtypeStruct((T, W), dt) for dt in dts] + [jax.ShapeDtypeStruct((T, GMLP_WIDTH), BF16)]
    out_specs = [pl.BlockSpec((tm, W), row)] * 7 + [pl.BlockSpec((tm, GMLP_WIDTH), row)]
    return pl.pallas_call(
        functools.partial(_mixin_kernel, tm=tm, tiles_per_seq=seq // tm),
        grid=(T // tm,),
        in_specs=in_specs,
        out_specs=out_specs,
        out_shape=out_shape,
        scratch_shapes=[pltpu.VMEM((1, D_SHIFT), F32)],
        compiler_params=pltpu.CompilerParams(
            dimension_semantics=("arbitrary",), vmem_limit_bytes=VMEM_LIMIT),
        name="mixin",
    )(x1, mix_g, w_in, mu, w0, a0, lora, g_up, k_k, k_a, ln_g, ln_b, sgu_w, sgu_bias)


def _dot_nn(a, b):
    return jnp.dot(a, b, preferred_element_type=F32)


def _dot_nt(a, b):
    return lax.dot_general(a, b, (((1,), (1,)), ((), ())), preferred_element_type=F32)


def _dot_tn(a, b):
    return lax.dot_general(a, b, (((0,), (0,)), ((), ())), preferred_element_type=F32)


def _cumsum_rows(tri, x):
    out = None
    for _ in range(3):
        hi = x.astype(BF16)
        t = _dot_nn(tri, hi)
        out = t if out is None else out + t
        x = x - hi.astype(F32)
    return out


def _scan_kernel(r_ref, k_ref, v_ref, na_ref, bb_ref, lw_ref, g_ref, gnw_ref, gnb_ref, rk_ref,
                 y_o, s_ref, yacc_ref, *, sb, nbat):
    C = SCAN_CHUNK
    n_pairs = RWKV_WIDTH // LANES
    sls = [slice(hp * LANES, (hp + 1) * LANES) for hp in range(n_pairs)]
    streams = [(b, hp) for b in range(nbat) for hp in range(n_pairs)]
    ids = range(len(streams))

    @pl.when(pl.program_id(1) == 0)
    def _():
        s_ref[...] = jnp.zeros_like(s_ref)

    lane = lax.broadcasted_iota(jnp.int32, (1, LANES), 1)
    lane_lo = lane < RWKV_HEAD
    ci = lax.broadcasted_iota(jnp.int32, (C, C), 0)
    cj = lax.broadcasted_iota(jnp.int32, (C, C), 1)
    tri = jnp.where(ci >= cj, 1.0, 0.0).astype(BF16)
    pt = lax.broadcasted_iota(jnp.int32, (C, LANES), 0)
    ps_ = lax.broadcasted_iota(jnp.int32, (C, LANES), 1) % C
    strict = ps_ < pt
    incl = ps_ <= pt
    eye = jnp.where(ps_ == pt, 1.0, 0.0).astype(F32)
    li = lax.broadcasted_iota(jnp.int32, (LANES, LANES), 0)
    lj = lax.broadcasted_iota(jnp.int32, (LANES, LANES), 1)
    same_head = (li // RWKV_HEAD) == (lj // RWKV_HEAD)

    def stack(x):
        return jnp.concatenate([jnp.where(lane_lo, x, 0.0), jnp.where(lane_lo, 0.0, x)], axis=0).astype(BF16)

    def prep(c):
        rows = slice(c * C, (c + 1) * C)
        x2, kb, kb_s, vs, d_end = [], [], [], [], []
        for b in range(nbat):
            lw = lw_ref[b, rows, :]
            cum = _cumsum_rows(tri, lw)
            e_pos = jnp.exp(cum)
            e_neg = jnp.exp(-cum)
            rt = r_ref[b, rows, :] * e_pos
            at = na_ref[b, rows, :] * jnp.exp(cum - lw)
            kh = k_ref[b, rows, :] * e_neg
            bh = bb_ref[b, rows, :] * e_neg
            vv = v_ref[b, rows, :]
            for sl in sls:
                x2.append(jnp.concatenate([at[:, sl], rt[:, sl]], axis=0).astype(BF16))
                kb.append(jnp.concatenate([kh[:, sl], bh[:, sl]], axis=0).astype(BF16))
                kb_s.append(jnp.concatenate([stack(kh[:, sl]), stack(bh[:, sl])], axis=0))
                vs.append((vv[:, sl].astype(BF16), stack(vv[:, sl])))
                d_end.append(e_pos[C - 1:C, sl])
        return x2, kb, kb_s, vs, d_end

    def head(ops, s):
        x2, kb, kb_s, vs, d_end = ops
        gram = [_dot_nt(x2[i], kb_s[i]) for i in ids]
        xs = [_dot_nt(x2[i], s[i].astype(BF16)) for i in ids]
        a_ab = [jnp.where(strict, gram[i][:C, LANES:], 0.0) for i in ids]
        a_kr = [jnp.concatenate([jnp.where(strict, gram[i][:C, :LANES], 0.0),
                                 jnp.where(incl, gram[i][C:, :LANES], 0.0)], axis=0).astype(BF16) for i in ids]
        a_rb = [jnp.where(incl, gram[i][C:, LANES:], 0.0).astype(BF16) for i in ids]
        av = [_dot_nn(a_kr[i], vs[i][1]) for i in ids]
        inv = [eye + a_ab[i] for i in ids]
        q = [_dot_nn(a_ab[i].astype(BF16), stack(a_ab[i])) for i in ids]
        for _ in range(int(math.log2(C)) - 2):
            t = [_dot_nn(jnp.concatenate([q[i], inv[i]], axis=0).astype(BF16), stack(q[i])) for i in ids]
            inv = [inv[i] + t[i][C:] for i in ids]
            q = [t[i][:C] for i in ids]
        inv = [inv[i] + _dot_nn(inv[i].astype(BF16), stack(q[i])) for i in ids]
        return inv, a_rb, xs, av

    def tail(c, ops, hd, s):
        x2, kb, kb_s, vs, d_end = ops
        inv, a_rb, xs, av = hd
        rows = slice(c * C, (c + 1) * C)
        u = [_dot_nn(inv[i].astype(BF16), stack(xs[i][:C] + av[i][:C])) for i in ids]
        yu = [_dot_nn(a_rb[i], stack(u[i])) for i in ids]
        upd = [_dot_tn(jnp.concatenate([vs[i][0], u[i].astype(BF16)], axis=0), kb[i])
               for i in ids]
        for i, (b, hp) in enumerate(streams):
            yacc_ref[b, rows, sls[hp]] = xs[i][C:] + av[i][C:] + yu[i]
        return [(s[i] + jnp.where(same_head, upd[i], 0.0)) * d_end[i] for i in ids]

    n_chunks = sb // C
    state = [s_ref[i] for i in ids]
    ops = prep(0)
    for c in range(n_chunks):
        hd = head(ops, state)
        nxt = prep(c + 1) if c + 1 < n_chunks else None
        state = tail(c, ops, hd, state)
        ops = nxt
    for i in ids:
        s_ref[i] = state[i]

    head_ones = jnp.where(same_head, 1.0, 0.0).astype(BF16)

    def head_sums2(x):
        hi = x.astype(BF16)
        lo = (x - hi.astype(F32)).astype(BF16)
        return _dot_nn(hi, head_ones) + _dot_nn(lo, head_ones)

    mean = [head_sums2(yacc_ref[b, :, sls[hp]]) * (1.0 / RWKV_HEAD) for b, hp in streams]
    rks = [_dot_nn((r_ref[b, :, sls[hp]].astype(F32) * k_ref[b, :, sls[hp]] * rk_ref[:, sls[hp]]).astype(BF16), head_ones)
           for b, hp in streams]
    d = [yacc_ref[b, :, sls[hp]] - mean[i] for i, (b, hp) in enumerate(streams)]
    var = [_dot_nn((d[i] * d[i]).astype(BF16), head_ones) * (1.0 / RWKV_HEAD) for i in ids]
    for i, (b, hp) in enumerate(streams):
        sl = sls[hp]
        yn = d[i] * lax.rsqrt(var[i] + GN_EPS) * gnw_ref[:, sl] + gnb_ref[:, sl]
        y_o[b, :, sl] = ((yn + rks[i] * v_ref[b, :, sl]) * g_ref[b, :, sl]).astype(BF16)


def _scan(r, k, v, na, bb, lw, g, gn_w, gn_b, r_k, *, batch, seq, sb, nbat):
    T, W = r.shape
    n_streams = nbat * (W // LANES)
    blk = lambda b, j: (b, j, 0)
    const = lambda b, j: (0, 0)
    in_specs = [pl.BlockSpec((nbat, sb, W), blk)] * 7 + [pl.BlockSpec((1, W), const)] * 3
    seq3 = lambda a: a.reshape(batch, seq, W)
    y = pl.pallas_call(
        functools.partial(_scan_kernel, sb=sb, nbat=nbat),
        grid=(batch // nbat, seq // sb),
        in_specs=in_specs,
        out_specs=pl.BlockSpec((nbat, sb, W), blk),
        out_shape=jax.ShapeDtypeStruct((batch, seq, W), BF16),
        scratch_shapes=[pltpu.VMEM((n_streams, LANES, LANES), F32), pltpu.VMEM((nbat, sb, W), F32)],
        compiler_params=pltpu.CompilerParams(
            dimension_semantics=("arbitrary", "arbitrary"), vmem_limit_bytes=VMEM_LIMIT),
        name="rwkv_scan",
    )(*(seq3(a) for a in (r, k, v, na, bb, lw, g)), gn_w, gn_b, r_k)
    return y.reshape(T, W)


def _tile(n, pref):
    return pref if n % pref == 0 else CHUNK


def kernel(x, ffn1_norm, ffn1_w1, ffn1_w3, ffn1_w2, mix_norm, w_in, mu_shift, w0, w_lora_up, a0, a_lora_up, g_lora_up, k_k, k_a, r_k, gn_w, gn_b, sgu_ln_g, sgu_ln_b, sgu_w, sgu_b, w_out, ffn2_norm, ffn2_w1, ffn2_w3, ffn2_w2, final_norm):
    B, S, D = x.shape
    T = B * S
    depth = ffn1_norm.shape[0]
    tm_ffn = _tile(S, 512)
    tm_mix = _tile(S, 512)
    sb = _tile(S, 512)
    rowv = lambda a: a.reshape(1, -1).astype(F32)

    xt = x.reshape(T, D)
    for l in range(depth):
        f1, f3, f2 = _narrow([ffn1_w1, ffn1_w3, ffn1_w2], l, n_steps=NARROW_STEPS)
        later = [w_in, w_out, ffn2_w1, ffn2_w3, ffn2_w2]
        xt, (w_in_b, w_out_b, w1_b, w3_b, w2_b) = _ffn(
            xt, rowv(ffn1_norm[l]), f1, f3, f2, tm=tm_ffn, narrow=later, layer=l)
        zeros = jnp.zeros((DECAY_LORA, RWKV_WIDTH), F32)
        lora = jnp.concatenate([jnp.concatenate([w_lora_up[l], zeros], axis=1),
                                jnp.concatenate([zeros, a_lora_up[l]], axis=1)], axis=0)
        sgu_bias = jnp.repeat(jnp.transpose(sgu_b[l]), GMLP_WIDTH // GMLP_GROUPS, axis=1)
        r, k, v, na, bb, lw, g, y_gmlp = _mixin(
            xt, rowv(mix_norm[l]), w_in_b, rowv(mu_shift[l]), rowv(w0[l]), rowv(a0[l]),
            lora.astype(BF16), g_lora_up[l].astype(BF16), rowv(k_k[l]), rowv(k_a[l]),
            rowv(sgu_ln_g[l]), rowv(sgu_ln_b[l]), sgu_w[l], sgu_bias, tm=tm_mix, seq=S)
        y_rwkv = _scan(r, k, v, na, bb, lw, g, rowv(gn_w[l]), rowv(gn_b[l]), rowv(r_k[l]),
                       batch=B, seq=S, sb=sb, nbat=SCAN_BATCH if B % SCAN_BATCH == 0 else 1)
        last = l == depth - 1
        xt, _ = _ffn(xt, rowv(ffn2_norm[l]), w1_b, w3_b, w2_b, tm=tm_ffn,
                     mix_in=(y_rwkv, y_gmlp, w_out_b),
                     final_g=rowv(final_norm) if last else None)
    if depth == 0:
        raise ValueError("depth must be positive")
    return xt.reshape(B, S, D)
```

```python
import functools
import math

import jax
import jax.numpy as jnp
from jax import lax
from jax.experimental import pallas as pl
from jax.experimental.pallas import tpu as pltpu

F32 = jnp.float32
BF16 = jnp.bfloat16

RWKV_WIDTH = 512
RWKV_HEAD = 64
GMLP_WIDTH = 512
GMLP_GROUPS = 8
CHUNK = 128
DECAY_LORA = 64
AAA_LORA = 64
GATE_LORA = 128
D_SHIFT = 3 * RWKV_WIDTH + DECAY_LORA + AAA_LORA + GATE_LORA
RMS_EPS = 1e-6
GN_EPS = 64e-5
LN_EPS = 1e-5
FFN_RES_SCALE = 0.5

LANES = 128
BF16_ROWS = 16
FFN_COLS = 1024
NARROW_STEPS = 8
SCAN_CHUNK = 64
SCAN_BATCH = 4
SUBLANES = 8
V7X_VMEM_BYTES = 64 * 1024 * 1024
VMEM_LIMIT = V7X_VMEM_BYTES * 7 // 8


def _rmsnorm(x, g):
    return x * lax.rsqrt(jnp.mean(x * x, axis=-1, keepdims=True) + RMS_EPS) * g


def _gelu(x):
    return 0.5 * x * (1.0 + lax.erf(x * (1.0 / math.sqrt(2.0))))


def _sigmoid(x):
    return 1.0 / (1.0 + jnp.exp(-x))


def _bdot(a, b):
    return jnp.dot(a.astype(BF16), b.astype(BF16), preferred_element_type=F32)


def _ffn_kernel(*refs, mix, final, n_cast):
    it = iter(refs)
    x_ref = next(it)
    if mix:
        yr_ref, yg_ref, wo_ref = next(it), next(it), next(it)
    g_ref, w1_ref, w3_ref, w2_ref = next(it), next(it), next(it), next(it)
    fg_ref = next(it) if final else None
    cast_in = [next(it) for _ in range(n_cast)]
    o_ref = next(it)
    cast_out = [next(it) for _ in range(n_cast)]

    x = x_ref[...]
    if mix:
        x = x + jnp.dot(yr_ref[...], wo_ref[:RWKV_WIDTH, :], preferred_element_type=F32)
        x = x + jnp.dot(yg_ref[...], wo_ref[RWKV_WIDTH:, :], preferred_element_type=F32)
    h = _rmsnorm(x, g_ref[...]).astype(BF16)
    d_ff = w1_ref.shape[1]
    acc = None
    for c0 in range(0, d_ff, FFN_COLS):
        c1 = min(c0 + FFN_COLS, d_ff)
        a = jnp.dot(h, w1_ref[:, c0:c1], preferred_element_type=F32)
        b = jnp.dot(h, w3_ref[:, c0:c1], preferred_element_type=F32)
        z = (a * _sigmoid(a) * b).astype(BF16)
        part = jnp.dot(z, w2_ref[c0:c1, :], preferred_element_type=F32)
        acc = part if acc is None else acc + part
    y = x + FFN_RES_SCALE * acc
    if final:
        y = _rmsnorm(y, fg_ref[...])
    o_ref[...] = y
    for src, dst in zip(cast_in, cast_out):
        dst[...] = src[...].astype(BF16)


def _narrow_kernel(*refs):
    half = len(refs) // 2
    for src, dst in zip(refs[:half], refs[half:]):
        dst[...] = src[...].astype(BF16)


def _row_blocks(n_rows, n_steps):
    k = 1
    while k <= n_steps:
        if (n_rows * k) % n_steps == 0 and n_steps % k == 0:
            rows = n_rows * k // n_steps
            if rows % BF16_ROWS == 0:
                return rows, k
        k *= 2
    return None


def _narrow_specs(arrs, layer, n_steps):
    specs = []
    for a in arrs:
        rows, k = _row_blocks(a.shape[1], n_steps)
        specs.append(pl.BlockSpec((None, rows, a.shape[2]), lambda i, k=k: (layer, i // k, 0)))
    return specs


def _narrow(arrs, layer, *, n_steps):
    return pl.pallas_call(
        _narrow_kernel,
        grid=(n_steps,),
        in_specs=_narrow_specs(arrs, layer, n_steps),
        out_specs=_narrow_specs(arrs, 0, n_steps),
        out_shape=[jax.ShapeDtypeStruct((1,) + a.shape[1:], BF16) for a in arrs],
        compiler_params=pltpu.CompilerParams(dimension_semantics=("arbitrary",), vmem_limit_bytes=VMEM_LIMIT),
        name="narrow_weights",
    )(*arrs)


def _whole(a, layered=False):
    if layered:
        return pl.BlockSpec((None,) + a.shape[1:], lambda i: (0, 0, 0), pipeline_mode=pl.Buffered(1))
    return pl.BlockSpec(a.shape, lambda i: (0,) * a.ndim, pipeline_mode=pl.Buffered(1))


def _ffn(x, norm_g, w1, w3, w2, *, tm, mix_in=None, final_g=None, narrow=(), layer=0):
    T, D = x.shape
    n = T // tm
    mix = mix_in is not None
    final = final_g is not None
    row = lambda i: (i, 0)
    whole = lambda a: _whole(a, layered=a.ndim == 3)
    in_specs = [pl.BlockSpec((tm, D), row)]
    args = [x]
    if mix:
        yr, yg, wo = mix_in
        in_specs += [pl.BlockSpec((tm, yr.shape[1]), row), pl.BlockSpec((tm, yg.shape[1]), row), whole(wo)]
        args += [yr, yg, wo]
    in_specs += [whole(norm_g), whole(w1), whole(w3), whole(w2)]
    args += [norm_g, w1, w3, w2]
    if final:
        in_specs.append(whole(final_g))
        args.append(final_g)
    out_specs = [pl.BlockSpec((tm, D), row)]
    out_shape = [jax.ShapeDtypeStruct((T, D), F32)]
    in_specs += _narrow_specs(narrow, layer, n)
    args += list(narrow)
    out_specs += _narrow_specs(narrow, 0, n)
    out_shape += [jax.ShapeDtypeStruct((1,) + a.shape[1:], BF16) for a in narrow]
    outs = pl.pallas_call(
        functools.partial(_ffn_kernel, mix=mix, final=final, n_cast=len(narrow)),
        grid=(n,),
        in_specs=in_specs,
        out_specs=out_specs,
        out_shape=out_shape,
        compiler_params=pltpu.CompilerParams(
            dimension_semantics=("arbitrary",), vmem_limit_bytes=VMEM_LIMIT),
        name="ffn_mix_final" if mix else "ffn",
    )(*args)
    return outs[0], list(outs[1:])


def _head_sums(x, lane_lo):
    s0 = jnp.sum(jnp.where(lane_lo, x, 0.0), axis=-1, keepdims=True)
    s1 = jnp.sum(jnp.where(lane_lo, 0.0, x), axis=-1, keepdims=True)
    return jnp.where(lane_lo, s0, s1)


def _mixin_kernel(x_ref, g_ref, win_ref, mu_ref, w0_ref, a0_ref, lora_ref, gup_ref, kk_ref, ka_ref,
                  lng_ref, lnb_ref, sw_ref, sb_ref,
                  r_o, k_o, v_o, na_o, bb_o, lw_o, g_o, ygm_o, prev_ref, *, tm, tiles_per_seq):
    i = pl.program_id(0)
    W = RWKV_WIDTH
    h = _rmsnorm(x_ref[...], g_ref[...]).astype(BF16)
    p = jnp.dot(h, win_ref[...], preferred_element_type=F32)

    ps = p[:, :D_SHIFT]
    rolled = pltpu.roll(ps, 1, 0)
    first = (i % tiles_per_seq) == 0
    prev = jnp.where(first, 0.0, prev_ref[...])
    row = lax.broadcasted_iota(jnp.int32, (SUBLANES, 1), 0)
    shifted = jnp.concatenate([jnp.where(row == 0, prev, rolled[:SUBLANES]), rolled[SUBLANES:]], axis=0)
    prev_ref[...] = ps[tm - 1:tm, :]
    ps = ps + (shifted - ps) * mu_ref[...]

    r = ps[:, :W]
    k = ps[:, W:2 * W]
    v = ps[:, 2 * W:3 * W]
    slab = ps[:, 3 * W:3 * W + DECAY_LORA + AAA_LORA]
    pg = ps[:, 3 * W + DECAY_LORA + AAA_LORA:]
    lane = lax.broadcasted_iota(jnp.int32, (1, LANES), 1)
    slab = jnp.where(lane < DECAY_LORA, jnp.tanh(slab), slab)
    lora = _bdot(slab, lora_ref[...])
    lw_o[...] = -math.exp(-0.5) * _sigmoid(w0_ref[...] + lora[:, :W])
    a = _sigmoid(a0_ref[...] + lora[:, W:])
    g_o[...] = _bdot(_sigmoid(pg), gup_ref[...])

    lane_lo = lane < RWKV_HEAD
    kk = k * kk_ref[...]
    for hp in range(W // LANES):
        sl = slice(hp * LANES, (hp + 1) * LANES)
        kkp = kk[:, sl]
        ss = _head_sums(kkp * kkp, lane_lo)
        kkn = kkp * jnp.minimum(lax.rsqrt(ss), 1e12)
        na_o[:, sl] = -kkn
        bb_o[:, sl] = kkn * a[:, sl]
    r_o[...] = r
    k_o[...] = k * (1.0 + (a - 1.0) * ka_ref[...])
    v_o[...] = v

    u = _gelu(p[:, D_SHIFT:D_SHIFT + GMLP_WIDTH])
    vg = _gelu(p[:, D_SHIFT + GMLP_WIDTH:])
    mean = jnp.mean(vg, axis=-1, keepdims=True)
    d = vg - mean
    var = jnp.mean(d * d, axis=-1, keepdims=True)
    vn = (d * lax.rsqrt(var + LN_EPS) * lng_ref[...] + lnb_ref[...]).astype(BF16)
    tr = lax.broadcasted_iota(jnp.int32, (CHUNK, CHUNK), 0)
    tc = lax.broadcasted_iota(jnp.int32, (CHUNK, CHUNK), 1)
    ws = [jnp.where(tr >= tc, sw_ref[gi], 0.0).astype(BF16) for gi in range(GMLP_GROUPS)]
    bias = sb_ref[...]
    for c in range(tm // CHUNK):
        rows = slice(c * CHUNK, (c + 1) * CHUNK)
        outs = []
        for hp in range(GMLP_WIDTH // LANES):
            vp = vn[rows, hp * LANES:(hp + 1) * LANES]
            m0 = jnp.dot(ws[2 * hp], vp, preferred_element_type=F32)
            m1 = jnp.dot(ws[2 * hp + 1], vp, preferred_element_type=F32)
            outs.append(jnp.where(lane_lo, m0, m1))
        mixed = jnp.concatenate(outs, axis=1) + bias
        ygm_o[rows, :] = (u[rows, :] * mixed).astype(BF16)


def _mixin(x1, mix_g, w_in, mu, w0, a0, lora, g_up, k_k, k_a, ln_g, ln_b, sgu_w, sgu_bias, *, tm, seq):
    T, D = x1.shape
    W = RWKV_WIDTH
    row = lambda i: (i, 0)
    const = lambda i: (0, 0)
    in_specs = [pl.BlockSpec((tm, D), row)] + [_whole(a, layered=a is w_in) for a in
                (mix_g, w_in, mu, w0, a0, lora, g_up, k_k, k_a, ln_g, ln_b, sgu_w, sgu_bias)]
    out_shape = [jax.ShapeDtypeStruct((T, W), F32)] * 7 + [jax.ShapeDtypeStruct((T, GMLP_WIDTH), BF16)]
    out_specs = [pl.BlockSpec((tm, W), row)] * 7 + [pl.BlockSpec((tm, GMLP_WIDTH), row)]
    return pl.pallas_call(
        functools.partial(_mixin_kernel, tm=tm, tiles_per_seq=seq // tm),
        grid=(T // tm,),
        in_specs=in_specs,
        out_specs=out_specs,
        out_shape=out_shape,
        scratch_shapes=[pltpu.VMEM((1, D_SHIFT), F32)],
        compiler_params=pltpu.CompilerParams(
            dimension_semantics=("arbitrary",), vmem_limit_bytes=VMEM_LIMIT),
        name="mixin",
    )(x1, mix_g, w_in, mu, w0, a0, lora, g_up, k_k, k_a, ln_g, ln_b, sgu_w, sgu_bias)


def _dot_nn(a, b):
    return jnp.dot(a, b, preferred_element_type=F32)


def _dot_nt(a, b):
    return lax.dot_general(a, b, (((1,), (1,)), ((), ())), preferred_element_type=F32)


def _dot_tn(a, b):
    return lax.dot_general(a, b, (((0,), (0,)), ((), ())), preferred_element_type=F32)


def _cumsum_rows(tri, x):
    out = None
    for _ in range(3):
        hi = x.astype(BF16)
        t = _dot_nn(tri, hi)
        out = t if out is None else out + t
        x = x - hi.astype(F32)
    return out


def _scan_kernel(r_ref, k_ref, v_ref, na_ref, bb_ref, lw_ref, g_ref, gnw_ref, gnb_ref, rk_ref,
                 y_o, s_ref, yacc_ref, *, sb, nbat):
    C = SCAN_CHUNK
    n_pairs = RWKV_WIDTH // LANES
    sls = [slice(hp * LANES, (hp + 1) * LANES) for hp in range(n_pairs)]
    streams = [(b, hp) for b in range(nbat) for hp in range(n_pairs)]
    ids = range(len(streams))

    @pl.when(pl.program_id(1) == 0)
    def _():
        s_ref[...] = jnp.zeros_like(s_ref)

    lane = lax.broadcasted_iota(jnp.int32, (1, LANES), 1)
    lane_lo = lane < RWKV_HEAD
    ci = lax.broadcasted_iota(jnp.int32, (C, C), 0)
    cj = lax.broadcasted_iota(jnp.int32, (C, C), 1)
    tri = jnp.where(ci >= cj, 1.0, 0.0).astype(BF16)
    pt = lax.broadcasted_iota(jnp.int32, (C, LANES), 0)
    ps_ = lax.broadcasted_iota(jnp.int32, (C, LANES), 1) % C
    strict = ps_ < pt
    incl = ps_ <= pt
    eye = jnp.where(ps_ == pt, 1.0, 0.0).astype(F32)
    li = lax.broadcasted_iota(jnp.int32, (LANES, LANES), 0)
    lj = lax.broadcasted_iota(jnp.int32, (LANES, LANES), 1)
    same_head = (li // RWKV_HEAD) == (lj // RWKV_HEAD)

    def stack(x):
        return jnp.concatenate([jnp.where(lane_lo, x, 0.0), jnp.where(lane_lo, 0.0, x)], axis=0).astype(BF16)

    def prep(c):
        rows = slice(c * C, (c + 1) * C)
        x2, kb, kb_s, vs, d_end = [], [], [], [], []
        for b in range(nbat):
            lw = lw_ref[b, rows, :]
            cum = _cumsum_rows(tri, lw)
            e_pos = jnp.exp(cum)
            e_neg = jnp.exp(-cum)
            rt = r_ref[b, rows, :] * e_pos
            at = na_ref[b, rows, :] * jnp.exp(cum - lw)
            kh = k_ref[b, rows, :] * e_neg
            bh = bb_ref[b, rows, :] * e_neg
            vv = v_ref[b, rows, :]
            for sl in sls:
                x2.append(jnp.concatenate([at[:, sl], rt[:, sl]], axis=0).astype(BF16))
                kb.append(jnp.concatenate([kh[:, sl], bh[:, sl]], axis=0).astype(BF16))
                kb_s.append(jnp.concatenate([stack(kh[:, sl]), stack(bh[:, sl])], axis=0))
                vs.append((vv[:, sl].astype(BF16), stack(vv[:, sl])))
                d_end.append(e_pos[C - 1:C, sl])
        return x2, kb, kb_s, vs, d_end

    def head(ops, s):
        x2, kb, kb_s, vs, d_end = ops
        gram = [_dot_nt(x2[i], kb_s[i]) for i in ids]
        xs = [_dot_nt(x2[i], s[i].astype(BF16)) for i in ids]
        a_ab = [jnp.where(strict, gram[i][:C, LANES:], 0.0) for i in ids]
        a_kr = [jnp.concatenate([jnp.where(strict, gram[i][:C, :LANES], 0.0),
                                 jnp.where(incl, gram[i][C:, :LANES], 0.0)], axis=0).astype(BF16) for i in ids]
        a_rb = [jnp.where(incl, gram[i][C:, LANES:], 0.0).astype(BF16) for i in ids]
        av = [_dot_nn(a_kr[i], vs[i][1]) for i in ids]
        inv = [eye + a_ab[i] for i in ids]
        q = [_dot_nn(a_ab[i].astype(BF16), stack(a_ab[i])) for i in ids]
        for _ in range(int(math.log2(C)) - 2):
            t = [_dot_nn(jnp.concatenate([q[i], inv[i]], axis=0).astype(BF16), stack(q[i])) for i in ids]
            inv = [inv[i] + t[i][C:] for i in ids]
            q = [t[i][:C] for i in ids]
        inv = [inv[i] + _dot_nn(inv[i].astype(BF16), stack(q[i])) for i in ids]
        return inv, a_rb, xs, av

    def tail(c, ops, hd, s):
        x2, kb, kb_s, vs, d_end = ops
        inv, a_rb, xs, av = hd
        rows = slice(c * C, (c + 1) * C)
        u = [_dot_nn(inv[i].astype(BF16), stack(xs[i][:C] + av[i][:C])) for i in ids]
        yu = [_dot_nn(a_rb[i], stack(u[i])) for i in ids]
        upd = [_dot_tn(jnp.concatenate([vs[i][0], u[i].astype(BF16)], axis=0), kb[i])
               for i in ids]
        for i, (b, hp) in enumerate(streams):
            yacc_ref[b, rows, sls[hp]] = xs[i][C:] + av[i][C:] + yu[i]
        return [(s[i] + jnp.where(same_head, upd[i], 0.0)) * d_end[i] for i in ids]

    n_chunks = sb // C
    state = [s_ref[i] for i in ids]
    ops = prep(0)
    for c in range(n_chunks):
        hd = head(ops, state)
        nxt = prep(c + 1) if c + 1 < n_chunks else None
        state = tail(c, ops, hd, state)
        ops = nxt
    for i in ids:
        s_ref[i] = state[i]

    head_ones = jnp.where(same_head, 1.0, 0.0).astype(BF16)

    def head_sums2(x):
        hi = x.astype(BF16)
        lo = (x - hi.astype(F32)).astype(BF16)
        return _dot_nn(hi, head_ones) + _dot_nn(lo, head_ones)

    mean = [head_sums2(yacc_ref[b, :, sls[hp]]) * (1.0 / RWKV_HEAD) for b, hp in streams]
    rks = [_dot_nn((r_ref[b, :, sls[hp]] * k_ref[b, :, sls[hp]] * rk_ref[:, sls[hp]]).astype(BF16), head_ones)
           for b, hp in streams]
    d = [yacc_ref[b, :, sls[hp]] - mean[i] for i, (b, hp) in enumerate(streams)]
    var = [_dot_nn((d[i] * d[i]).astype(BF16), head_ones) * (1.0 / RWKV_HEAD) for i in ids]
    for i, (b, hp) in enumerate(streams):
        sl = sls[hp]
        yn = d[i] * lax.rsqrt(var[i] + GN_EPS) * gnw_ref[:, sl] + gnb_ref[:, sl]
        y_o[b, :, sl] = ((yn + rks[i] * v_ref[b, :, sl]) * g_ref[b, :, sl]).astype(BF16)


def _scan(r, k, v, na, bb, lw, g, gn_w, gn_b, r_k, *, batch, seq, sb, nbat):
    T, W = r.shape
    n_streams = nbat * (W // LANES)
    blk = lambda b, j: (b, j, 0)
    const = lambda b, j: (0, 0)
    in_specs = [pl.BlockSpec((nbat, sb, W), blk)] * 7 + [pl.BlockSpec((1, W), const)] * 3
    seq3 = lambda a: a.reshape(batch, seq, W)
    y = pl.pallas_call(
        functools.partial(_scan_kernel, sb=sb, nbat=nbat),
        grid=(batch // nbat, seq // sb),
        in_specs=in_specs,
        out_specs=pl.BlockSpec((nbat, sb, W), blk),
        out_shape=jax.ShapeDtypeStruct((batch, seq, W), BF16),
        scratch_shapes=[pltpu.VMEM((n_streams, LANES, LANES), F32), pltpu.VMEM((nbat, sb, W), F32)],
        compiler_params=pltpu.CompilerParams(
            dimension_semantics=("arbitrary", "arbitrary"), vmem_limit_bytes=VMEM_LIMIT),
        name="rwkv_scan",
    )(*(seq3(a) for a in (r, k, v, na, bb, lw, g)), gn_w, gn_b, r_k)
    return y.reshape(T, W)


def _tile(n, pref):
    return pref if n % pref == 0 else CHUNK


def kernel(x, ffn1_norm, ffn1_w1, ffn1_w3, ffn1_w2, mix_norm, w_in, mu_shift, w0, w_lora_up, a0, a_lora_up, g_lora_up, k_k, k_a, r_k, gn_w, gn_b, sgu_ln_g, sgu_ln_b, sgu_w, sgu_b, w_out, ffn2_norm, ffn2_w1, ffn2_w3, ffn2_w2, final_norm):
    B, S, D = x.shape
    T = B * S
    depth = ffn1_norm.shape[0]
    tm_ffn = _tile(S, 1024)
    tm_mix = _tile(S, 512)
    sb = _tile(S, 256)
    rowv = lambda a: a.reshape(1, -1).astype(F32)

    xt = x.reshape(T, D)
    for l in range(depth):
        f1, f3, f2 = _narrow([ffn1_w1, ffn1_w3, ffn1_w2], l, n_steps=NARROW_STEPS)
        later = [w_in, w_out, ffn2_w1, ffn2_w3, ffn2_w2]
        xt, (w_in_b, w_out_b, w1_b, w3_b, w2_b) = _ffn(
            xt, rowv(ffn1_norm[l]), f1, f3, f2, tm=tm_ffn, narrow=later, layer=l)
        zeros = jnp.zeros((DECAY_LORA, RWKV_WIDTH), F32)
        lora = jnp.concatenate([jnp.concatenate([w_lora_up[l], zeros], axis=1),
                                jnp.concatenate([zeros, a_lora_up[l]], axis=1)], axis=0)
        sgu_bias = jnp.repeat(jnp.transpose(sgu_b[l]), GMLP_WIDTH // GMLP_GROUPS, axis=1)
        r, k, v, na, bb, lw, g, y_gmlp = _mixin(
            xt, rowv(mix_norm[l]), w_in_b, rowv(mu_shift[l]), rowv(w0[l]), rowv(a0[l]),
            lora.astype(BF16), g_lora_up[l].astype(BF16), rowv(k_k[l]), rowv(k_a[l]),
            rowv(sgu_ln_g[l]), rowv(sgu_ln_b[l]), sgu_w[l], sgu_bias, tm=tm_mix, seq=S)
        y_rwkv = _scan(r, k, v, na, bb, lw, g, rowv(gn_w[l]), rowv(gn_b[l]), rowv(r_k[l]),
                       batch=B, seq=S, sb=sb, nbat=SCAN_BATCH if B % SCAN_BATCH == 0 else 1)
        last = l == depth - 1
        xt, _ = _ffn(xt, rowv(ffn2_norm[l]), w1_b, w3_b, w2_b, tm=tm_ffn,
                     mix_in=(y_rwkv, y_gmlp, w_out_b),
                     final_g=rowv(final_norm) if last else None)
    if depth == 0:
        raise ValueError("depth must be positive")
    return xt.reshape(B, S, D)
```

```python
import functools
import math

import jax
import jax.numpy as jnp
from jax import lax
from jax.experimental import pallas as pl
from jax.experimental.pallas import tpu as pltpu

F32 = jnp.float32
BF16 = jnp.bfloat16

RWKV_WIDTH = 512
RWKV_HEAD = 64
GMLP_WIDTH = 512
GMLP_GROUPS = 8
CHUNK = 128
DECAY_LORA = 64
AAA_LORA = 64
GATE_LORA = 128
D_SHIFT = 3 * RWKV_WIDTH + DECAY_LORA + AAA_LORA + GATE_LORA
RMS_EPS = 1e-6
GN_EPS = 64e-5
LN_EPS = 1e-5
FFN_RES_SCALE = 0.5

LANES = 128
BF16_ROWS = 16
FFN_COLS = 1024
NARROW_STEPS = 4
SCAN_CHUNK = 64
SCAN_BATCH = 4
SUBLANES = 8
V7X_VMEM_BYTES = 64 * 1024 * 1024
VMEM_LIMIT = V7X_VMEM_BYTES * 7 // 8


def _rmsnorm(x, g):
    return x * lax.rsqrt(jnp.mean(x * x, axis=-1, keepdims=True) + RMS_EPS) * g


def _gelu(x):
    return 0.5 * x * (1.0 + lax.erf(x * (1.0 / math.sqrt(2.0))))


def _sigmoid(x):
    return 1.0 / (1.0 + jnp.exp(-x))


def _bdot(a, b):
    return jnp.dot(a.astype(BF16), b.astype(BF16), preferred_element_type=F32)


def _ffn_kernel(*refs, mix, final, n_cast):
    it = iter(refs)
    x_ref = next(it)
    if mix:
        yr_ref, yg_ref, wo_ref = next(it), next(it), next(it)
    g_ref, w1_ref, w3_ref, w2_ref = next(it), next(it), next(it), next(it)
    fg_ref = next(it) if final else None
    cast_in = [next(it) for _ in range(n_cast)]
    o_ref = next(it)
    cast_out = [next(it) for _ in range(n_cast)]

    x = x_ref[...]
    if mix:
        x = x + jnp.dot(yr_ref[...], wo_ref[:RWKV_WIDTH, :], preferred_element_type=F32)
        x = x + jnp.dot(yg_ref[...], wo_ref[RWKV_WIDTH:, :], preferred_element_type=F32)
    h = _rmsnorm(x, g_ref[...]).astype(BF16)
    d_ff = w1_ref.shape[1]
    acc = None
    for c0 in range(0, d_ff, FFN_COLS):
        c1 = min(c0 + FFN_COLS, d_ff)
        a = jnp.dot(h, w1_ref[:, c0:c1], preferred_element_type=F32)
        b = jnp.dot(h, w3_ref[:, c0:c1], preferred_element_type=F32)
        z = (a * _sigmoid(a) * b).astype(BF16)
        part = jnp.dot(z, w2_ref[c0:c1, :], preferred_element_type=F32)
        acc = part if acc is None else acc + part
    y = x + FFN_RES_SCALE * acc
    if final:
        y = _rmsnorm(y, fg_ref[...])
    o_ref[...] = y
    for src, dst in zip(cast_in, cast_out):
        dst[...] = src[...].astype(BF16)


def _narrow_kernel(*refs):
    half = len(refs) // 2
    for src, dst in zip(refs[:half], refs[half:]):
        dst[...] = src[...].astype(BF16)


def _row_blocks(n_rows, n_steps):
    k = 1
    while k <= n_steps:
        if (n_rows * k) % n_steps == 0 and n_steps % k == 0:
            rows = n_rows * k // n_steps
            if rows % BF16_ROWS == 0:
                return rows, k
        k *= 2
    return None


def _narrow_specs(arrs, layer, n_steps):
    specs = []
    for a in arrs:
        rows, k = _row_blocks(a.shape[1], n_steps)
        specs.append(pl.BlockSpec((None, rows, a.shape[2]), lambda i, k=k: (layer, i // k, 0)))
    return specs


def _narrow(arrs, layer, *, n_steps):
    return pl.pallas_call(
        _narrow_kernel,
        grid=(n_steps,),
        in_specs=_narrow_specs(arrs, layer, n_steps),
        out_specs=_narrow_specs(arrs, 0, n_steps),
        out_shape=[jax.ShapeDtypeStruct((1,) + a.shape[1:], BF16) for a in arrs],
        compiler_params=pltpu.CompilerParams(dimension_semantics=("arbitrary",), vmem_limit_bytes=VMEM_LIMIT),
        name="narrow_weights",
    )(*arrs)


def _whole(a, layered=False):
    if layered:
        return pl.BlockSpec((None,) + a.shape[1:], lambda i: (0, 0, 0), pipeline_mode=pl.Buffered(1))
    return pl.BlockSpec(a.shape, lambda i: (0,) * a.ndim, pipeline_mode=pl.Buffered(1))


def _ffn(x, norm_g, w1, w3, w2, *, tm, mix_in=None, final_g=None, narrow=(), layer=0):
    T, D = x.shape
    n = T // tm
    mix = mix_in is not None
    final = final_g is not None
    row = lambda i: (i, 0)
    whole = lambda a: _whole(a, layered=a.ndim == 3)
    in_specs = [pl.BlockSpec((tm, D), row)]
    args = [x]
    if mix:
        yr, yg, wo = mix_in
        in_specs += [pl.BlockSpec((tm, yr.shape[1]), row), pl.BlockSpec((tm, yg.shape[1]), row), whole(wo)]
        args += [yr, yg, wo]
    in_specs += [whole(norm_g), whole(w1), whole(w3), whole(w2)]
    args += [norm_g, w1, w3, w2]
    if final:
        in_specs.append(whole(final_g))
        args.append(final_g)
    out_specs = [pl.BlockSpec((tm, D), row)]
    out_shape = [jax.ShapeDtypeStruct((T, D), F32)]
    in_specs += _narrow_specs(narrow, layer, n)
    args += list(narrow)
    out_specs += _narrow_specs(narrow, 0, n)
    out_shape += [jax.ShapeDtypeStruct((1,) + a.shape[1:], BF16) for a in narrow]
    outs = pl.pallas_call(
        functools.partial(_ffn_kernel, mix=mix, final=final, n_cast=len(narrow)),
        grid=(n,),
        in_specs=in_specs,
        out_specs=out_specs,
        out_shape=out_shape,
        compiler_params=pltpu.CompilerParams(
            dimension_semantics=("arbitrary",), vmem_limit_bytes=VMEM_LIMIT),
        name="ffn_mix_final" if mix else "ffn",
    )(*args)
    return outs[0], list(outs[1:])


def _head_sums(x, lane_lo):
    s0 = jnp.sum(jnp.where(lane_lo, x, 0.0), axis=-1, keepdims=True)
    s1 = jnp.sum(jnp.where(lane_lo, 0.0, x), axis=-1, keepdims=True)
    return jnp.where(lane_lo, s0, s1)


def _mixin_kernel(x_ref, g_ref, win_ref, mu_ref, w0_ref, a0_ref, lora_ref, gup_ref, kk_ref, ka_ref,
                  lng_ref, lnb_ref, sw_ref, sb_ref,
                  r_o, k_o, v_o, na_o, bb_o, lw_o, g_o, ygm_o, prev_ref, *, tm, tiles_per_seq):
    i = pl.program_id(0)
    W = RWKV_WIDTH
    h = _rmsnorm(x_ref[...], g_ref[...]).astype(BF16)
    p = jnp.dot(h, win_ref[...], preferred_element_type=F32)

    ps = p[:, :D_SHIFT]
    rolled = pltpu.roll(ps, 1, 0)
    first = (i % tiles_per_seq) == 0
    prev = jnp.where(first, 0.0, prev_ref[...])
    row = lax.broadcasted_iota(jnp.int32, (SUBLANES, 1), 0)
    shifted = jnp.concatenate([jnp.where(row == 0, prev, rolled[:SUBLANES]), rolled[SUBLANES:]], axis=0)
    prev_ref[...] = ps[tm - 1:tm, :]
    ps = ps + (shifted - ps) * mu_ref[...]

    r = ps[:, :W]
    k = ps[:, W:2 * W]
    v = ps[:, 2 * W:3 * W]
    slab = ps[:, 3 * W:3 * W + DECAY_LORA + AAA_LORA]
    pg = ps[:, 3 * W + DECAY_LORA + AAA_LORA:]
    lane = lax.broadcasted_iota(jnp.int32, (1, LANES), 1)
    slab = jnp.where(lane < DECAY_LORA, jnp.tanh(slab), slab)
    lora = _bdot(slab, lora_ref[...])
    lw_o[...] = -math.exp(-0.5) * _sigmoid(w0_ref[...] + lora[:, :W])
    a = _sigmoid(a0_ref[...] + lora[:, W:])
    g_o[...] = _bdot(_sigmoid(pg), gup_ref[...])

    lane_lo = lane < RWKV_HEAD
    kk = k * kk_ref[...]
    for hp in range(W // LANES):
        sl = slice(hp * LANES, (hp + 1) * LANES)
        kkp = kk[:, sl]
        ss = _head_sums(kkp * kkp, lane_lo)
        kkn = kkp * jnp.minimum(lax.rsqrt(ss), 1e12)
        na_o[:, sl] = -kkn
        bb_o[:, sl] = kkn * a[:, sl]
    r_o[...] = r
    k_o[...] = k * (1.0 + (a - 1.0) * ka_ref[...])
    v_o[...] = v

    u = _gelu(p[:, D_SHIFT:D_SHIFT + GMLP_WIDTH])
    vg = _gelu(p[:, D_SHIFT + GMLP_WIDTH:])
    mean = jnp.mean(vg, axis=-1, keepdims=True)
    d = vg - mean
    var = jnp.mean(d * d, axis=-1, keepdims=True)
    vn = (d * lax.rsqrt(var + LN_EPS) * lng_ref[...] + lnb_ref[...]).astype(BF16)
    tr = lax.broadcasted_iota(jnp.int32, (CHUNK, CHUNK), 0)
    tc = lax.broadcasted_iota(jnp.int32, (CHUNK, CHUNK), 1)
    ws = [jnp.where(tr >= tc, sw_ref[gi], 0.0).astype(BF16) for gi in range(GMLP_GROUPS)]
    bias = sb_ref[...]
    for c in range(tm // CHUNK):
        rows = slice(c * CHUNK, (c + 1) * CHUNK)
        outs = []
        for hp in range(GMLP_WIDTH // LANES):
            vp = vn[rows, hp * LANES:(hp + 1) * LANES]
            m0 = jnp.dot(ws[2 * hp], vp, preferred_element_type=F32)
            m1 = jnp.dot(ws[2 * hp + 1], vp, preferred_element_type=F32)
            outs.append(jnp.where(lane_lo, m0, m1))
        mixed = jnp.concatenate(outs, axis=1) + bias
        ygm_o[rows, :] = (u[rows, :] * mixed).astype(BF16)


def _mixin(x1, mix_g, w_in, mu, w0, a0, lora, g_up, k_k, k_a, ln_g, ln_b, sgu_w, sgu_bias, *, tm, seq):
    T, D = x1.shape
    W = RWKV_WIDTH
    row = lambda i: (i, 0)
    const = lambda i: (0, 0)
    in_specs = [pl.BlockSpec((tm, D), row)] + [_whole(a, layered=a is w_in) for a in
                (mix_g, w_in, mu, w0, a0, lora, g_up, k_k, k_a, ln_g, ln_b, sgu_w, sgu_bias)]
    out_shape = [jax.ShapeDtypeStruct((T, W), F32)] * 7 + [jax.ShapeDtypeStruct((T, GMLP_WIDTH), BF16)]
    out_specs = [pl.BlockSpec((tm, W), row)] * 7 + [pl.BlockSpec((tm, GMLP_WIDTH), row)]
    return pl.pallas_call(
        functools.partial(_mixin_kernel, tm=tm, tiles_per_seq=seq // tm),
        grid=(T // tm,),
        in_specs=in_specs,
        out_specs=out_specs,
        out_shape=out_shape,
        scratch_shapes=[pltpu.VMEM((1, D_SHIFT), F32)],
        compiler_params=pltpu.CompilerParams(
            dimension_semantics=("arbitrary",), vmem_limit_bytes=VMEM_LIMIT),
        name="mixin",
    )(x1, mix_g, w_in, mu, w0, a0, lora, g_up, k_k, k_a, ln_g, ln_b, sgu_w, sgu_bias)


def _dot_nn(a, b):
    return jnp.dot(a, b, preferred_element_type=F32)


def _dot_nt(a, b):
    return lax.dot_general(a, b, (((1,), (1,)), ((), ())), preferred_element_type=F32)


def _dot_tn(a, b):
    return lax.dot_general(a, b, (((0,), (0,)), ((), ())), preferred_element_type=F32)


def _cumsum_rows(tri, x):
    out = None
    for _ in range(3):
        hi = x.astype(BF16)
        t = _dot_nn(tri, hi)
        out = t if out is None else out + t
        x = x - hi.astype(F32)
    return out


def _scan_kernel(r_ref, k_ref, v_ref, na_ref, bb_ref, lw_ref, g_ref, gnw_ref, gnb_ref, rk_ref,
                 y_o, s_ref, yacc_ref, *, sb, nbat):
    C = SCAN_CHUNK
    n_pairs = RWKV_WIDTH // LANES
    sls = [slice(hp * LANES, (hp + 1) * LANES) for hp in range(n_pairs)]
    streams = [(b, hp) for b in range(nbat) for hp in range(n_pairs)]
    ids = range(len(streams))

    @pl.when(pl.program_id(1) == 0)
    def _():
        s_ref[...] = jnp.zeros_like(s_ref)

    lane = lax.broadcasted_iota(jnp.int32, (1, LANES), 1)
    lane_lo = lane < RWKV_HEAD
    ci = lax.broadcasted_iota(jnp.int32, (C, C), 0)
    cj = lax.broadcasted_iota(jnp.int32, (C, C), 1)
    tri = jnp.where(ci >= cj, 1.0, 0.0).astype(BF16)
    pt = lax.broadcasted_iota(jnp.int32, (C, LANES), 0)
    ps_ = lax.broadcasted_iota(jnp.int32, (C, LANES), 1) % C
    strict = ps_ < pt
    incl = ps_ <= pt
    eye = jnp.where(ps_ == pt, 1.0, 0.0).astype(F32)
    li = lax.broadcasted_iota(jnp.int32, (LANES, LANES), 0)
    lj = lax.broadcasted_iota(jnp.int32, (LANES, LANES), 1)
    same_head = (li // RWKV_HEAD) == (lj // RWKV_HEAD)

    def stack(x):
        return jnp.concatenate([jnp.where(lane_lo, x, 0.0), jnp.where(lane_lo, 0.0, x)], axis=0).astype(BF16)

    def prep(c):
        rows = slice(c * C, (c + 1) * C)
        x2, kb, kb_s, vs, d_end = [], [], [], [], []
        for b in range(nbat):
            lw = lw_ref[b, rows, :]
            cum = _cumsum_rows(tri, lw)
            e_pos = jnp.exp(cum)
            e_neg = jnp.exp(-cum)
            rt = r_ref[b, rows, :] * e_pos
            at = na_ref[b, rows, :] * jnp.exp(cum - lw)
            kh = k_ref[b, rows, :] * e_neg
            bh = bb_ref[b, rows, :] * e_neg
            vv = v_ref[b, rows, :]
            for sl in sls:
                x2.append(jnp.concatenate([at[:, sl], rt[:, sl]], axis=0).astype(BF16))
                kb.append(jnp.concatenate([kh[:, sl], bh[:, sl]], axis=0).astype(BF16))
                kb_s.append(jnp.concatenate([stack(kh[:, sl]), stack(bh[:, sl])], axis=0))
                vs.append((vv[:, sl].astype(BF16), stack(vv[:, sl])))
                d_end.append(e_pos[C - 1:C, sl])
        return x2, kb, kb_s, vs, d_end

    def head(ops, s):
        x2, kb, kb_s, vs, d_end = ops
        gram = [_dot_nt(x2[i], kb_s[i]) for i in ids]
        xs = [_dot_nt(x2[i], s[i].astype(BF16)) for i in ids]
        a_ab = [jnp.where(strict, gram[i][:C, LANES:], 0.0) for i in ids]
        a_kr = [jnp.concatenate([jnp.where(strict, gram[i][:C, :LANES], 0.0),
                                 jnp.where(incl, gram[i][C:, :LANES], 0.0)], axis=0).astype(BF16) for i in ids]
        a_rb = [jnp.where(incl, gram[i][C:, LANES:], 0.0).astype(BF16) for i in ids]
        av = [_dot_nn(a_kr[i], vs[i][1]) for i in ids]
        inv = [eye + a_ab[i] for i in ids]
        q = [_dot_nn(a_ab[i].astype(BF16), stack(a_ab[i])) for i in ids]
        for _ in range(int(math.log2(C)) - 2):
            t = [_dot_nn(jnp.concatenate([q[i], inv[i]], axis=0).astype(BF16), stack(q[i])) for i in ids]
            inv = [inv[i] + t[i][C:] for i in ids]
            q = [t[i][:C] for i in ids]
        inv = [inv[i] + _dot_nn(inv[i].astype(BF16), stack(q[i])) for i in ids]
        return inv, a_rb, xs, av

    def tail(c, ops, hd, s):
        x2, kb, kb_s, vs, d_end = ops
        inv, a_rb, xs, av = hd
        rows = slice(c * C, (c + 1) * C)
        u = [_dot_nn(inv[i].astype(BF16), stack(xs[i][:C] + av[i][:C])) for i in ids]
        yu = [_dot_nn(a_rb[i], stack(u[i])) for i in ids]
        upd = [_dot_tn(jnp.concatenate([vs[i][0], u[i].astype(BF16)], axis=0), kb[i])
               for i in ids]
        for i, (b, hp) in enumerate(streams):
            yacc_ref[b, rows, sls[hp]] = xs[i][C:] + av[i][C:] + yu[i]
        return [(s[i] + jnp.where(same_head, upd[i], 0.0)) * d_end[i] for i in ids]

    n_chunks = sb // C
    state = [s_ref[i] for i in ids]
    ops = prep(0)
    for c in range(n_chunks):
        hd = head(ops, state)
        nxt = prep(c + 1) if c + 1 < n_chunks else None
        state = tail(c, ops, hd, state)
        ops = nxt
    for i in ids:
        s_ref[i] = state[i]

    head_ones = jnp.where(same_head, 1.0, 0.0).astype(BF16)

    def head_sums2(x):
        hi = x.astype(BF16)
        lo = (x - hi.astype(F32)).astype(BF16)
        return _dot_nn(hi, head_ones) + _dot_nn(lo, head_ones)

    mean = [head_sums2(yacc_ref[b, :, sls[hp]]) * (1.0 / RWKV_HEAD) for b, hp in streams]
    rks = [_dot_nn((r_ref[b, :, sls[hp]] * k_ref[b, :, sls[hp]] * rk_ref[:, sls[hp]]).astype(BF16), head_ones)
           for b, hp in streams]
    d = [yacc_ref[b, :, sls[hp]] - mean[i] for i, (b, hp) in enumerate(streams)]
    var = [_dot_nn((d[i] * d[i]).astype(BF16), head_ones) * (1.0 / RWKV_HEAD) for i in ids]
    for i, (b, hp) in enumerate(streams):
        sl = sls[hp]
        yn = d[i] * lax.rsqrt(var[i] + GN_EPS) * gnw_ref[:, sl] + gnb_ref[:, sl]
        y_o[b, :, sl] = ((yn + rks[i] * v_ref[b, :, sl]) * g_ref[b, :, sl]).astype(BF16)


def _scan(r, k, v, na, bb, lw, g, gn_w, gn_b, r_k, *, batch, seq, sb, nbat):
    T, W = r.shape
    n_streams = nbat * (W // LANES)
    blk = lambda b, j: (b, j, 0)
    const = lambda b, j: (0, 0)
    in_specs = [pl.BlockSpec((nbat, sb, W), blk)] * 7 + [pl.BlockSpec((1, W), const)] * 3
    seq3 = lambda a: a.reshape(batch, seq, W)
    y = pl.pallas_call(
        functools.partial(_scan_kernel, sb=sb, nbat=nbat),
        grid=(batch // nbat, seq // sb),
        in_specs=in_specs,
        out_specs=pl.BlockSpec((nbat, sb, W), blk),
        out_shape=jax.ShapeDtypeStruct((batch, seq, W), BF16),
        scratch_shapes=[pltpu.VMEM((n_streams, LANES, LANES), F32), pltpu.VMEM((nbat, sb, W), F32)],
        compiler_params=pltpu.CompilerParams(
            dimension_semantics=("arbitrary", "arbitrary"), vmem_limit_bytes=VMEM_LIMIT),
        name="rwkv_scan",
    )(*(seq3(a) for a in (r, k, v, na, bb, lw, g)), gn_w, gn_b, r_k)
    return y.reshape(T, W)


def _tile(n, pref):
    return pref if n % pref == 0 else CHUNK


def kernel(x, ffn1_norm, ffn1_w1, ffn1_w3, ffn1_w2, mix_norm, w_in, mu_shift, w0, w_lora_up, a0, a_lora_up, g_lora_up, k_k, k_a, r_k, gn_w, gn_b, sgu_ln_g, sgu_ln_b, sgu_w, sgu_b, w_out, ffn2_norm, ffn2_w1, ffn2_w3, ffn2_w2, final_norm):
    B, S, D = x.shape
    T = B * S
    depth = ffn1_norm.shape[0]
    tm_ffn = _tile(S, 1024)
    tm_mix = _tile(S, 1024)
    sb = _tile(S, 256)
    rowv = lambda a: a.reshape(1, -1).astype(F32)

    xt = x.reshape(T, D)
    for l in range(depth):
        f1, f3, f2 = _narrow([ffn1_w1, ffn1_w3, ffn1_w2], l, n_steps=NARROW_STEPS)
        later = [w_in, w_out, ffn2_w1, ffn2_w3, ffn2_w2]
        xt, (w_in_b, w_out_b, w1_b, w3_b, w2_b) = _ffn(
            xt, rowv(ffn1_norm[l]), f1, f3, f2, tm=tm_ffn, narrow=later, layer=l)
        zeros = jnp.zeros((DECAY_LORA, RWKV_WIDTH), F32)
        lora = jnp.concatenate([jnp.concatenate([w_lora_up[l], zeros], axis=1),
                                jnp.concatenate([zeros, a_lora_up[l]], axis=1)], axis=0)
        sgu_bias = jnp.repeat(jnp.transpose(sgu_b[l]), GMLP_WIDTH // GMLP_GROUPS, axis=1)
        r, k, v, na, bb, lw, g, y_gmlp = _mixin(
            xt, rowv(mix_norm[l]), w_in_b, rowv(mu_shift[l]), rowv(w0[l]), rowv(a0[l]),
            lora.astype(BF16), g_lora_up[l].astype(BF16), rowv(k_k[l]), rowv(k_a[l]),
            rowv(sgu_ln_g[l]), rowv(sgu_ln_b[l]), sgu_w[l], sgu_bias, tm=tm_mix, seq=S)
        y_rwkv = _scan(r, k, v, na, bb, lw, g, rowv(gn_w[l]), rowv(gn_b[l]), rowv(r_k[l]),
                       batch=B, seq=S, sb=sb, nbat=SCAN_BATCH if B % SCAN_BATCH == 0 else 1)
        last = l == depth - 1
        xt, _ = _ffn(xt, rowv(ffn2_norm[l]), w1_b, w3_b, w2_b, tm=tm_ffn,
                     mix_in=(y_rwkv, y_gmlp, w_out_b),
                     final_g=rowv(final_norm) if last else None)
    if depth == 0:
        raise ValueError("depth must be positive")
    return xt.reshape(B, S, D)
```

```python
import functools
import math

import jax
import jax.numpy as jnp
from jax import lax
from jax.experimental import pallas as pl
from jax.experimental.pallas import tpu as pltpu

F32 = jnp.float32
BF16 = jnp.bfloat16

RWKV_WIDTH = 512
RWKV_HEAD = 64
GMLP_WIDTH = 512
GMLP_GROUPS = 8
CHUNK = 128
DECAY_LORA = 64
AAA_LORA = 64
GATE_LORA = 128
D_SHIFT = 3 * RWKV_WIDTH + DECAY_LORA + AAA_LORA + GATE_LORA
RMS_EPS = 1e-6
GN_EPS = 64e-5
LN_EPS = 1e-5
FFN_RES_SCALE = 0.5

LANES = 128
BF16_ROWS = 16
FFN_COLS = 1024
NARROW_STEPS = 4
SCAN_CHUNK = 64
SCAN_BATCH = 4
SUBLANES = 8
V7X_VMEM_BYTES = 64 * 1024 * 1024
VMEM_LIMIT = V7X_VMEM_BYTES * 7 // 8


def _rmsnorm(x, g):
    return x * lax.rsqrt(jnp.mean(x * x, axis=-1, keepdims=True) + RMS_EPS) * g


def _gelu(x):
    return 0.5 * x * (1.0 + lax.erf(x * (1.0 / math.sqrt(2.0))))


def _sigmoid(x):
    return 1.0 / (1.0 + jnp.exp(-x))


def _bdot(a, b):
    return jnp.dot(a.astype(BF16), b.astype(BF16), preferred_element_type=F32)


def _ffn_kernel(*refs, mix, final, n_cast):
    it = iter(refs)
    x_ref = next(it)
    if mix:
        yr_ref, yg_ref, wo_ref = next(it), next(it), next(it)
    g_ref, w1_ref, w3_ref, w2_ref = next(it), next(it), next(it), next(it)
    fg_ref = next(it) if final else None
    cast_in = [next(it) for _ in range(n_cast)]
    o_ref = next(it)
    cast_out = [next(it) for _ in range(n_cast)]

    x = x_ref[...]
    if mix:
        x = x + jnp.dot(yr_ref[...], wo_ref[:RWKV_WIDTH, :], preferred_element_type=F32)
        x = x + jnp.dot(yg_ref[...], wo_ref[RWKV_WIDTH:, :], preferred_element_type=F32)
    h = _rmsnorm(x, g_ref[...]).astype(BF16)
    d_ff = w1_ref.shape[1]
    acc = None
    for c0 in range(0, d_ff, FFN_COLS):
        c1 = min(c0 + FFN_COLS, d_ff)
        a = jnp.dot(h, w1_ref[:, c0:c1], preferred_element_type=F32)
        b = jnp.dot(h, w3_ref[:, c0:c1], preferred_element_type=F32)
        z = (a * _sigmoid(a) * b).astype(BF16)
        part = jnp.dot(z, w2_ref[c0:c1, :], preferred_element_type=F32)
        acc = part if acc is None else acc + part
    y = x + FFN_RES_SCALE * acc
    if final:
        y = _rmsnorm(y, fg_ref[...])
    o_ref[...] = y
    for src, dst in zip(cast_in, cast_out):
        dst[...] = src[...].astype(BF16)


def _narrow_kernel(*refs):
    half = len(refs) // 2
    for src, dst in zip(refs[:half], refs[half:]):
        dst[...] = src[...].astype(BF16)


def _row_blocks(n_rows, n_steps):
    k = 1
    while k <= n_steps:
        if (n_rows * k) % n_steps == 0 and n_steps % k == 0:
            rows = n_rows * k // n_steps
            if rows % BF16_ROWS == 0:
                return rows, k
        k *= 2
    return None


def _narrow_specs(arrs, layer, n_steps):
    specs = []
    for a in arrs:
        rows, k = _row_blocks(a.shape[1], n_steps)
        specs.append(pl.BlockSpec((None, rows, a.shape[2]), lambda i, k=k: (layer, i // k, 0)))
    return specs


def _narrow(arrs, layer, *, n_steps):
    return pl.pallas_call(
        _narrow_kernel,
        grid=(n_steps,),
        in_specs=_narrow_specs(arrs, layer, n_steps),
        out_specs=_narrow_specs(arrs, 0, n_steps),
        out_shape=[jax.ShapeDtypeStruct((1,) + a.shape[1:], BF16) for a in arrs],
        compiler_params=pltpu.CompilerParams(dimension_semantics=("arbitrary",), vmem_limit_bytes=VMEM_LIMIT),
        name="narrow_weights",
    )(*arrs)


def _whole(a, layered=False):
    if layered:
        return pl.BlockSpec((None,) + a.shape[1:], lambda i: (0, 0, 0), pipeline_mode=pl.Buffered(1))
    return pl.BlockSpec(a.shape, lambda i: (0,) * a.ndim, pipeline_mode=pl.Buffered(1))


def _ffn(x, norm_g, w1, w3, w2, *, tm, mix_in=None, final_g=None, narrow=(), layer=0):
    T, D = x.shape
    n = T // tm
    mix = mix_in is not None
    final = final_g is not None
    row = lambda i: (i, 0)
    whole = lambda a: _whole(a, layered=a.ndim == 3)
    in_specs = [pl.BlockSpec((tm, D), row)]
    args = [x]
    if mix:
        yr, yg, wo = mix_in
        in_specs += [pl.BlockSpec((tm, yr.shape[1]), row), pl.BlockSpec((tm, yg.shape[1]), row), whole(wo)]
        args += [yr, yg, wo]
    in_specs += [whole(norm_g), whole(w1), whole(w3), whole(w2)]
    args += [norm_g, w1, w3, w2]
    if final:
        in_specs.append(whole(final_g))
        args.append(final_g)
    out_specs = [pl.BlockSpec((tm, D), row)]
    out_shape = [jax.ShapeDtypeStruct((T, D), F32)]
    in_specs += _narrow_specs(narrow, layer, n)
    args += list(narrow)
    out_specs += _narrow_specs(narrow, 0, n)
    out_shape += [jax.ShapeDtypeStruct((1,) + a.shape[1:], BF16) for a in narrow]
    outs = pl.pallas_call(
        functools.partial(_ffn_kernel, mix=mix, final=final, n_cast=len(narrow)),
        grid=(n,),
        in_specs=in_specs,
        out_specs=out_specs,
        out_shape=out_shape,
        compiler_params=pltpu.CompilerParams(
            dimension_semantics=("arbitrary",), vmem_limit_bytes=VMEM_LIMIT),
        name="ffn_mix_final" if mix else "ffn",
    )(*args)
    return outs[0], list(outs[1:])


def _head_sums(x, lane_lo):
    s0 = jnp.sum(jnp.where(lane_lo, x, 0.0), axis=-1, keepdims=True)
    s1 = jnp.sum(jnp.where(lane_lo, 0.0, x), axis=-1, keepdims=True)
    return jnp.where(lane_lo, s0, s1)


def _mixin_kernel(x_ref, g_ref, win_ref, mu_ref, w0_ref, a0_ref, lora_ref, gup_ref, kk_ref, ka_ref,
                  lng_ref, lnb_ref, sw_ref, sb_ref,
                  r_o, k_o, v_o, na_o, bb_o, lw_o, g_o, ygm_o, prev_ref, *, tm, tiles_per_seq):
    i = pl.program_id(0)
    W = RWKV_WIDTH
    h = _rmsnorm(x_ref[...], g_ref[...]).astype(BF16)
    p = jnp.dot(h, win_ref[...], preferred_element_type=F32)

    ps = p[:, :D_SHIFT]
    rolled = pltpu.roll(ps, 1, 0)
    first = (i % tiles_per_seq) == 0
    prev = jnp.where(first, 0.0, prev_ref[...])
    row = lax.broadcasted_iota(jnp.int32, (SUBLANES, 1), 0)
    shifted = jnp.concatenate([jnp.where(row == 0, prev, rolled[:SUBLANES]), rolled[SUBLANES:]], axis=0)
    prev_ref[...] = ps[tm - 1:tm, :]
    ps = ps + (shifted - ps) * mu_ref[...]

    r = ps[:, :W]
    k = ps[:, W:2 * W]
    v = ps[:, 2 * W:3 * W]
    slab = ps[:, 3 * W:3 * W + DECAY_LORA + AAA_LORA]
    pg = ps[:, 3 * W + DECAY_LORA + AAA_LORA:]
    lane = lax.broadcasted_iota(jnp.int32, (1, LANES), 1)
    slab = jnp.where(lane < DECAY_LORA, jnp.tanh(slab), slab)
    lora = _bdot(slab, lora_ref[...])
    lw_o[...] = -math.exp(-0.5) * _sigmoid(w0_ref[...] + lora[:, :W])
    a = _sigmoid(a0_ref[...] + lora[:, W:])
    g_o[...] = _bdot(_sigmoid(pg), gup_ref[...])

    lane_lo = lane < RWKV_HEAD
    kk = k * kk_ref[...]
    for hp in range(W // LANES):
        sl = slice(hp * LANES, (hp + 1) * LANES)
        kkp = kk[:, sl]
        ss = _head_sums(kkp * kkp, lane_lo)
        kkn = kkp * jnp.minimum(lax.rsqrt(ss), 1e12)
        na_o[:, sl] = -kkn
        bb_o[:, sl] = kkn * a[:, sl]
    r_o[...] = r
    k_o[...] = k * (1.0 + (a - 1.0) * ka_ref[...])
    v_o[...] = v

    u = _gelu(p[:, D_SHIFT:D_SHIFT + GMLP_WIDTH])
    vg = _gelu(p[:, D_SHIFT + GMLP_WIDTH:])
    mean = jnp.mean(vg, axis=-1, keepdims=True)
    d = vg - mean
    var = jnp.mean(d * d, axis=-1, keepdims=True)
    vn = (d * lax.rsqrt(var + LN_EPS) * lng_ref[...] + lnb_ref[...]).astype(BF16)
    tr = lax.broadcasted_iota(jnp.int32, (CHUNK, CHUNK), 0)
    tc = lax.broadcasted_iota(jnp.int32, (CHUNK, CHUNK), 1)
    ws = [jnp.where(tr >= tc, sw_ref[gi], 0.0).astype(BF16) for gi in range(GMLP_GROUPS)]
    bias = sb_ref[...]
    for c in range(tm // CHUNK):
        rows = slice(c * CHUNK, (c + 1) * CHUNK)
        outs = []
        for hp in range(GMLP_WIDTH // LANES):
            vp = vn[rows, hp * LANES:(hp + 1) * LANES]
            m0 = jnp.dot(ws[2 * hp], vp, preferred_element_type=F32)
            m1 = jnp.dot(ws[2 * hp + 1], vp, preferred_element_type=F32)
            outs.append(jnp.where(lane_lo, m0, m1))
        mixed = jnp.concatenate(outs, axis=1) + bias
        ygm_o[rows, :] = (u[rows, :] * mixed).astype(BF16)


def _mixin(x1, mix_g, w_in, mu, w0, a0, lora, g_up, k_k, k_a, ln_g, ln_b, sgu_w, sgu_bias, *, tm, seq):
    T, D = x1.shape
    W = RWKV_WIDTH
    row = lambda i: (i, 0)
    const = lambda i: (0, 0)
    in_specs = [pl.BlockSpec((tm, D), row)] + [_whole(a, layered=a is w_in) for a in
                (mix_g, w_in, mu, w0, a0, lora, g_up, k_k, k_a, ln_g, ln_b, sgu_w, sgu_bias)]
    out_shape = [jax.ShapeDtypeStruct((T, W), F32)] * 7 + [jax.ShapeDtypeStruct((T, GMLP_WIDTH), BF16)]
    out_specs = [pl.BlockSpec((tm, W), row)] * 7 + [pl.BlockSpec((tm, GMLP_WIDTH), row)]
    return pl.pallas_call(
        functools.partial(_mixin_kernel, tm=tm, tiles_per_seq=seq // tm),
        grid=(T // tm,),
        in_specs=in_specs,
        out_specs=out_specs,
        out_shape=out_shape,
        scratch_shapes=[pltpu.VMEM((1, D_SHIFT), F32)],
        compiler_params=pltpu.CompilerParams(
            dimension_semantics=("arbitrary",), vmem_limit_bytes=VMEM_LIMIT),
        name="mixin",
    )(x1, mix_g, w_in, mu, w0, a0, lora, g_up, k_k, k_a, ln_g, ln_b, sgu_w, sgu_bias)


def _dot_nn(a, b):
    return jnp.dot(a, b, preferred_element_type=F32)


def _dot_nt(a, b):
    return lax.dot_general(a, b, (((1,), (1,)), ((), ())), preferred_element_type=F32)


def _dot_tn(a, b):
    return lax.dot_general(a, b, (((0,), (0,)), ((), ())), preferred_element_type=F32)


def _cumsum_rows(tri, x):
    out = None
    for _ in range(3):
        hi = x.astype(BF16)
        t = _dot_nn(tri, hi)
        out = t if out is None else out + t
        x = x - hi.astype(F32)
    return out


def _scan_kernel(r_ref, k_ref, v_ref, na_ref, bb_ref, lw_ref, g_ref, gnw_ref, gnb_ref, rk_ref,
                 y_o, s_ref, yacc_ref, *, sb, nbat):
    C = SCAN_CHUNK
    n_pairs = RWKV_WIDTH // LANES
    sls = [slice(hp * LANES, (hp + 1) * LANES) for hp in range(n_pairs)]
    streams = [(b, hp) for b in range(nbat) for hp in range(n_pairs)]
    ids = range(len(streams))

    @pl.when(pl.program_id(1) == 0)
    def _():
        s_ref[...] = jnp.zeros_like(s_ref)

    lane = lax.broadcasted_iota(jnp.int32, (1, LANES), 1)
    lane_lo = lane < RWKV_HEAD
    ci = lax.broadcasted_iota(jnp.int32, (C, C), 0)
    cj = lax.broadcasted_iota(jnp.int32, (C, C), 1)
    tri = jnp.where(ci >= cj, 1.0, 0.0).astype(BF16)
    pt = lax.broadcasted_iota(jnp.int32, (C, LANES), 0)
    ps_ = lax.broadcasted_iota(jnp.int32, (C, LANES), 1) % C
    strict = ps_ < pt
    incl = ps_ <= pt
    eye = jnp.where(ps_ == pt, 1.0, 0.0).astype(F32)
    li = lax.broadcasted_iota(jnp.int32, (LANES, LANES), 0)
    lj = lax.broadcasted_iota(jnp.int32, (LANES, LANES), 1)
    same_head = (li // RWKV_HEAD) == (lj // RWKV_HEAD)

    def stack(x):
        return jnp.concatenate([jnp.where(lane_lo, x, 0.0), jnp.where(lane_lo, 0.0, x)], axis=0).astype(BF16)

    def prep(c):
        rows = slice(c * C, (c + 1) * C)
        x2, kb, kb_s, vs, d_end = [], [], [], [], []
        for b in range(nbat):
            lw = lw_ref[b, rows, :]
            cum = _cumsum_rows(tri, lw)
            e_pos = jnp.exp(cum)
            e_neg = jnp.exp(-cum)
            rt = r_ref[b, rows, :] * e_pos
            at = na_ref[b, rows, :] * jnp.exp(cum - lw)
            kh = k_ref[b, rows, :] * e_neg
            bh = bb_ref[b, rows, :] * e_neg
            vv = v_ref[b, rows, :]
            for sl in sls:
                x2.append(jnp.concatenate([at[:, sl], rt[:, sl]], axis=0).astype(BF16))
                kb.append(jnp.concatenate([kh[:, sl], bh[:, sl]], axis=0).astype(BF16))
                kb_s.append(jnp.concatenate([stack(kh[:, sl]), stack(bh[:, sl])], axis=0))
                vs.append((vv[:, sl].astype(BF16), stack(vv[:, sl])))
                d_end.append(e_pos[C - 1:C, sl])
        return x2, kb, kb_s, vs, d_end

    def head(ops, s):
        x2, kb, kb_s, vs, d_end = ops
        gram = [_dot_nt(x2[i], kb_s[i]) for i in ids]
        xs = [_dot_nn(x2[i], s[i].astype(BF16)) for i in ids]
        a_ab = [jnp.where(strict, gram[i][:C, LANES:], 0.0) for i in ids]
        a_kr = [jnp.concatenate([jnp.where(strict, gram[i][:C, :LANES], 0.0),
                                 jnp.where(incl, gram[i][C:, :LANES], 0.0)], axis=0).astype(BF16) for i in ids]
        a_rb = [jnp.where(incl, gram[i][C:, LANES:], 0.0).astype(BF16) for i in ids]
        av = [_dot_nn(a_kr[i], vs[i][1]) for i in ids]
        inv = [eye + a_ab[i] for i in ids]
        q = [_dot_nn(a_ab[i].astype(BF16), stack(a_ab[i])) for i in ids]
        for _ in range(int(math.log2(C)) - 2):
            t = [_dot_nn(jnp.concatenate([q[i], inv[i]], axis=0).astype(BF16), stack(q[i])) for i in ids]
            inv = [inv[i] + t[i][C:] for i in ids]
            q = [t[i][:C] for i in ids]
        inv = [inv[i] + _dot_nn(inv[i].astype(BF16), stack(q[i])) for i in ids]
        return inv, a_rb, xs, av

    def tail(c, ops, hd, s):
        x2, kb, kb_s, vs, d_end = ops
        inv, a_rb, xs, av = hd
        rows = slice(c * C, (c + 1) * C)
        u = [_dot_nn(inv[i].astype(BF16), stack(xs[i][:C] + av[i][:C])) for i in ids]
        yu = [_dot_nn(a_rb[i], stack(u[i])) for i in ids]
        upd = [_dot_tn(kb[i], jnp.concatenate([vs[i][0], u[i].astype(BF16)], axis=0))
               for i in ids]
        for i, (b, hp) in enumerate(streams):
            yacc_ref[b, rows, sls[hp]] = xs[i][C:] + av[i][C:] + yu[i]
        decay = [jnp.transpose(jnp.broadcast_to(d_end[i], (LANES, LANES))) for i in ids]
        return [(s[i] + jnp.where(same_head, upd[i], 0.0)) * decay[i] for i in ids]

    n_chunks = sb // C
    state = [s_ref[i] for i in ids]
    ops = prep(0)
    for c in range(n_chunks):
        hd = head(ops, state)
        nxt = prep(c + 1) if c + 1 < n_chunks else None
        state = tail(c, ops, hd, state)
        ops = nxt
    for i in ids:
        s_ref[i] = state[i]

    head_ones = jnp.where(same_head, 1.0, 0.0).astype(BF16)

    def head_sums2(x):
        hi = x.astype(BF16)
        lo = (x - hi.astype(F32)).astype(BF16)
        return _dot_nn(hi, head_ones) + _dot_nn(lo, head_ones)

    mean = [head_sums2(yacc_ref[b, :, sls[hp]]) * (1.0 / RWKV_HEAD) for b, hp in streams]
    rks = [_dot_nn((r_ref[b, :, sls[hp]] * k_ref[b, :, sls[hp]] * rk_ref[:, sls[hp]]).astype(BF16), head_ones)
           for b, hp in streams]
    d = [yacc_ref[b, :, sls[hp]] - mean[i] for i, (b, hp) in enumerate(streams)]
    var = [_dot_nn((d[i] * d[i]).astype(BF16), head_ones) * (1.0 / RWKV_HEAD) for i in ids]
    for i, (b, hp) in enumerate(streams):
        sl = sls[hp]
        yn = d[i] * lax.rsqrt(var[i] + GN_EPS) * gnw_ref[:, sl] + gnb_ref[:, sl]
        y_o[b, :, sl] = ((yn + rks[i] * v_ref[b, :, sl]) * g_ref[b, :, sl]).astype(BF16)


def _scan(r, k, v, na, bb, lw, g, gn_w, gn_b, r_k, *, batch, seq, sb, nbat):
    T, W = r.shape
    n_streams = nbat * (W // LANES)
    blk = lambda b, j: (b, j, 0)
    const = lambda b, j: (0, 0)
    in_specs = [pl.BlockSpec((nbat, sb, W), blk)] * 7 + [pl.BlockSpec((1, W), const)] * 3
    seq3 = lambda a: a.reshape(batch, seq, W)
    y = pl.pallas_call(
        functools.partial(_scan_kernel, sb=sb, nbat=nbat),
        grid=(batch // nbat, seq // sb),
        in_specs=in_specs,
        out_specs=pl.BlockSpec((nbat, sb, W), blk),
        out_shape=jax.ShapeDtypeStruct((batch, seq, W), BF16),
        scratch_shapes=[pltpu.VMEM((n_streams, LANES, LANES), F32), pltpu.VMEM((nbat, sb, W), F32)],
        compiler_params=pltpu.CompilerParams(
            dimension_semantics=("arbitrary", "arbitrary"), vmem_limit_bytes=VMEM_LIMIT),
        name="rwkv_scan",
    )(*(seq3(a) for a in (r, k, v, na, bb, lw, g)), gn_w, gn_b, r_k)
    return y.reshape(T, W)


def _tile(n, pref):
    return pref if n % pref == 0 else CHUNK


def kernel(x, ffn1_norm, ffn1_w1, ffn1_w3, ffn1_w2, mix_norm, w_in, mu_shift, w0, w_lora_up, a0, a_lora_up, g_lora_up, k_k, k_a, r_k, gn_w, gn_b, sgu_ln_g, sgu_ln_b, sgu_w, sgu_b, w_out, ffn2_norm, ffn2_w1, ffn2_w3, ffn2_w2, final_norm):
    B, S, D = x.shape
    T = B * S
    depth = ffn1_norm.shape[0]
    tm_ffn = _tile(S, 1024)
    tm_mix = _tile(S, 1024)
    sb = _tile(S, 256)
    rowv = lambda a: a.reshape(1, -1).astype(F32)

    xt = x.reshape(T, D)
    for l in range(depth):
        f1, f3, f2 = _narrow([ffn1_w1, ffn1_w3, ffn1_w2], l, n_steps=NARROW_STEPS)
        later = [w_in, w_out, ffn2_w1, ffn2_w3, ffn2_w2]
        xt, (w_in_b, w_out_b, w1_b, w3_b, w2_b) = _ffn(
            xt, rowv(ffn1_norm[l]), f1, f3, f2, tm=tm_ffn, narrow=later, layer=l)
        zeros = jnp.zeros((DECAY_LORA, RWKV_WIDTH), F32)
        lora = jnp.concatenate([jnp.concatenate([w_lora_up[l], zeros], axis=1),
                                jnp.concatenate([zeros, a_lora_up[l]], axis=1)], axis=0)
        sgu_bias = jnp.repeat(jnp.transpose(sgu_b[l]), GMLP_WIDTH // GMLP_GROUPS, axis=1)
        r, k, v, na, bb, lw, g, y_gmlp = _mixin(
            xt, rowv(mix_norm[l]), w_in_b, rowv(mu_shift[l]), rowv(w0[l]), rowv(a0[l]),
            lora.astype(BF16), g_lora_up[l].astype(BF16), rowv(k_k[l]), rowv(k_a[l]),
            rowv(sgu_ln_g[l]), rowv(sgu_ln_b[l]), sgu_w[l], sgu_bias, tm=tm_mix, seq=S)
        y_rwkv = _scan(r, k, v, na, bb, lw, g, rowv(gn_w[l]), rowv(gn_b[l]), rowv(r_k[l]),
                       batch=B, seq=S, sb=sb, nbat=SCAN_BATCH if B % SCAN_BATCH == 0 else 1)
        last = l == depth - 1
        xt, _ = _ffn(xt, rowv(ffn2_norm[l]), w1_b, w3_b, w2_b, tm=tm_ffn,
                     mix_in=(y_rwkv, y_gmlp, w_out_b),
                     final_g=rowv(final_norm) if last else None)
    if depth == 0:
        raise ValueError("depth must be positive")
    return xt.reshape(B, S, D)
```

```python
import functools
import math

import jax
import jax.numpy as jnp
from jax import lax
from jax.experimental import pallas as pl
from jax.experimental.pallas import tpu as pltpu

F32 = jnp.float32
BF16 = jnp.bfloat16

RWKV_WIDTH = 512
RWKV_HEAD = 64
GMLP_WIDTH = 512
GMLP_GROUPS = 8
CHUNK = 128
DECAY_LORA = 64
AAA_LORA = 64
GATE_LORA = 128
D_SHIFT = 3 * RWKV_WIDTH + DECAY_LORA + AAA_LORA + GATE_LORA
RMS_EPS = 1e-6
GN_EPS = 64e-5
LN_EPS = 1e-5
FFN_RES_SCALE = 0.5

LANES = 128
BF16_ROWS = 16
FFN_COLS = 1024
NARROW_STEPS = 4
SCAN_CHUNK = 64
SCAN_BATCH = 4
SUBLANES = 8
V7X_VMEM_BYTES = 64 * 1024 * 1024
VMEM_LIMIT = V7X_VMEM_BYTES * 7 // 8


def _rmsnorm(x, g):
    return x * lax.rsqrt(jnp.mean(x * x, axis=-1, keepdims=True) + RMS_EPS) * g


def _gelu(x):
    return 0.5 * x * (1.0 + lax.erf(x * (1.0 / math.sqrt(2.0))))


def _sigmoid(x):
    return 1.0 / (1.0 + jnp.exp(-x))


def _bdot(a, b):
    return jnp.dot(a.astype(BF16), b.astype(BF16), preferred_element_type=F32)


def _ffn_kernel(*refs, mix, final, n_cast):
    it = iter(refs)
    x_ref = next(it)
    if mix:
        yr_ref, yg_ref, wo_ref = next(it), next(it), next(it)
    g_ref, w1_ref, w3_ref, w2_ref = next(it), next(it), next(it), next(it)
    fg_ref = next(it) if final else None
    cast_in = [next(it) for _ in range(n_cast)]
    o_ref = next(it)
    cast_out = [next(it) for _ in range(n_cast)]

    x = x_ref[...]
    if mix:
        x = x + jnp.dot(yr_ref[...], wo_ref[:RWKV_WIDTH, :], preferred_element_type=F32)
        x = x + jnp.dot(yg_ref[...], wo_ref[RWKV_WIDTH:, :], preferred_element_type=F32)
    h = _rmsnorm(x, g_ref[...]).astype(BF16)
    d_ff = w1_ref.shape[1]
    acc = None
    for c0 in range(0, d_ff, FFN_COLS):
        c1 = min(c0 + FFN_COLS, d_ff)
        a = jnp.dot(h, w1_ref[:, c0:c1], preferred_element_type=F32)
        b = jnp.dot(h, w3_ref[:, c0:c1], preferred_element_type=F32)
        z = (a * _sigmoid(a) * b).astype(BF16)
        part = jnp.dot(z, w2_ref[c0:c1, :], preferred_element_type=F32)
        acc = part if acc is None else acc + part
    y = x + FFN_RES_SCALE * acc
    if final:
        y = _rmsnorm(y, fg_ref[...])
    o_ref[...] = y
    for src, dst in zip(cast_in, cast_out):
        dst[...] = src[...].astype(BF16)


def _narrow_kernel(*refs):
    half = len(refs) // 2
    for src, dst in zip(refs[:half], refs[half:]):
        dst[...] = src[...].astype(BF16)


def _row_blocks(n_rows, n_steps):
    k = 1
    while k <= n_steps:
        if (n_rows * k) % n_steps == 0 and n_steps % k == 0:
            rows = n_rows * k // n_steps
            if rows % BF16_ROWS == 0:
                return rows, k
        k *= 2
    return None


def _narrow_specs(arrs, layer, n_steps):
    specs = []
    for a in arrs:
        rows, k = _row_blocks(a.shape[1], n_steps)
        specs.append(pl.BlockSpec((None, rows, a.shape[2]), lambda i, k=k: (layer, i // k, 0)))
    return specs


def _narrow(arrs, layer, *, n_steps):
    return pl.pallas_call(
        _narrow_kernel,
        grid=(n_steps,),
        in_specs=_narrow_specs(arrs, layer, n_steps),
        out_specs=_narrow_specs(arrs, 0, n_steps),
        out_shape=[jax.ShapeDtypeStruct((1,) + a.shape[1:], BF16) for a in arrs],
        compiler_params=pltpu.CompilerParams(dimension_semantics=("arbitrary",), vmem_limit_bytes=VMEM_LIMIT),
        name="narrow_weights",
    )(*arrs)


def _whole(a, layered=False):
    if layered:
        return pl.BlockSpec((None,) + a.shape[1:], lambda i: (0, 0, 0), pipeline_mode=pl.Buffered(1))
    return pl.BlockSpec(a.shape, lambda i: (0,) * a.ndim, pipeline_mode=pl.Buffered(1))


def _ffn(x, norm_g, w1, w3, w2, *, tm, mix_in=None, final_g=None, narrow=(), layer=0):
    T, D = x.shape
    n = T // tm
    mix = mix_in is not None
    final = final_g is not None
    row = lambda i: (i, 0)
    whole = lambda a: _whole(a, layered=a.ndim == 3)
    in_specs = [pl.BlockSpec((tm, D), row)]
    args = [x]
    if mix:
        yr, yg, wo = mix_in
        in_specs += [pl.BlockSpec((tm, yr.shape[1]), row), pl.BlockSpec((tm, yg.shape[1]), row), whole(wo)]
        args += [yr, yg, wo]
    in_specs += [whole(norm_g), whole(w1), whole(w3), whole(w2)]
    args += [norm_g, w1, w3, w2]
    if final:
        in_specs.append(whole(final_g))
        args.append(final_g)
    out_specs = [pl.BlockSpec((tm, D), row)]
    out_shape = [jax.ShapeDtypeStruct((T, D), F32)]
    in_specs += _narrow_specs(narrow, layer, n)
    args += list(narrow)
    out_specs += _narrow_specs(narrow, 0, n)
    out_shape += [jax.ShapeDtypeStruct((1,) + a.shape[1:], BF16) for a in narrow]
    outs = pl.pallas_call(
        functools.partial(_ffn_kernel, mix=mix, final=final, n_cast=len(narrow)),
        grid=(n,),
        in_specs=in_specs,
        out_specs=out_specs,
        out_shape=out_shape,
        compiler_params=pltpu.CompilerParams(
            dimension_semantics=("arbitrary",), vmem_limit_bytes=VMEM_LIMIT),
        name="ffn_mix_final" if mix else "ffn",
    )(*args)
    return outs[0], list(outs[1:])


def _head_sums(x, lane_lo):
    s0 = jnp.sum(jnp.where(lane_lo, x, 0.0), axis=-1, keepdims=True)
    s1 = jnp.sum(jnp.where(lane_lo, 0.0, x), axis=-1, keepdims=True)
    return jnp.where(lane_lo, s0, s1)


def _mixin_kernel(x_ref, g_ref, win_ref, mu_ref, w0_ref, a0_ref, lora_ref, gup_ref, kk_ref, ka_ref,
                  lng_ref, lnb_ref, sw_ref, sb_ref,
                  r_o, k_o, v_o, na_o, bb_o, lw_o, g_o, ygm_o, prev_ref, *, tm, tiles_per_seq):
    i = pl.program_id(0)
    W = RWKV_WIDTH
    h = _rmsnorm(x_ref[...], g_ref[...]).astype(BF16)
    p = jnp.dot(h, win_ref[...], preferred_element_type=F32)

    ps = p[:, :D_SHIFT]
    rolled = pltpu.roll(ps, 1, 0)
    first = (i % tiles_per_seq) == 0
    prev = jnp.where(first, 0.0, prev_ref[...])
    row = lax.broadcasted_iota(jnp.int32, (SUBLANES, 1), 0)
    shifted = jnp.concatenate([jnp.where(row == 0, prev, rolled[:SUBLANES]), rolled[SUBLANES:]], axis=0)
    prev_ref[...] = ps[tm - 1:tm, :]
    ps = ps + (shifted - ps) * mu_ref[...]

    r = ps[:, :W]
    k = ps[:, W:2 * W]
    v = ps[:, 2 * W:3 * W]
    slab = ps[:, 3 * W:3 * W + DECAY_LORA + AAA_LORA]
    pg = ps[:, 3 * W + DECAY_LORA + AAA_LORA:]
    lane = lax.broadcasted_iota(jnp.int32, (1, LANES), 1)
    slab = jnp.where(lane < DECAY_LORA, jnp.tanh(slab), slab)
    lora = _bdot(slab, lora_ref[...])
    lw_o[...] = -math.exp(-0.5) * _sigmoid(w0_ref[...] + lora[:, :W])
    a = _sigmoid(a0_ref[...] + lora[:, W:])
    g_o[...] = _bdot(_sigmoid(pg), gup_ref[...])

    lane_lo = lane < RWKV_HEAD
    kk = k * kk_ref[...]
    for hp in range(W // LANES):
        sl = slice(hp * LANES, (hp + 1) * LANES)
        kkp = kk[:, sl]
        ss = _head_sums(kkp * kkp, lane_lo)
        kkn = kkp * jnp.minimum(lax.rsqrt(ss), 1e12)
        na_o[:, sl] = -kkn
        bb_o[:, sl] = kkn * a[:, sl]
    r_o[...] = r
    k_o[...] = k * (1.0 + (a - 1.0) * ka_ref[...])
    v_o[...] = v

    u = _gelu(p[:, D_SHIFT:D_SHIFT + GMLP_WIDTH])
    vg = _gelu(p[:, D_SHIFT + GMLP_WIDTH:])
    mean = jnp.mean(vg, axis=-1, keepdims=True)
    d = vg - mean
    var = jnp.mean(d * d, axis=-1, keepdims=True)
    vn = (d * lax.rsqrt(var + LN_EPS) * lng_ref[...] + lnb_ref[...]).astype(BF16)
    tr = lax.broadcasted_iota(jnp.int32, (CHUNK, CHUNK), 0)
    tc = lax.broadcasted_iota(jnp.int32, (CHUNK, CHUNK), 1)
    ws = [jnp.where(tr >= tc, sw_ref[gi], 0.0).astype(BF16) for gi in range(GMLP_GROUPS)]
    bias = sb_ref[...]
    for c in range(tm // CHUNK):
        rows = slice(c * CHUNK, (c + 1) * CHUNK)
        outs = []
        for hp in range(GMLP_WIDTH // LANES):
            vp = vn[rows, hp * LANES:(hp + 1) * LANES]
            m0 = jnp.dot(ws[2 * hp], vp, preferred_element_type=F32)
            m1 = jnp.dot(ws[2 * hp + 1], vp, preferred_element_type=F32)
            outs.append(jnp.where(lane_lo, m0, m1))
        mixed = jnp.concatenate(outs, axis=1) + bias
        ygm_o[rows, :] = (u[rows, :] * mixed).astype(BF16)


def _mixin(x1, mix_g, w_in, mu, w0, a0, lora, g_up, k_k, k_a, ln_g, ln_b, sgu_w, sgu_bias, *, tm, seq):
    T, D = x1.shape
    W = RWKV_WIDTH
    row = lambda i: (i, 0)
    const = lambda i: (0, 0)
    in_specs = [pl.BlockSpec((tm, D), row)] + [_whole(a, layered=a is w_in) for a in
                (mix_g, w_in, mu, w0, a0, lora, g_up, k_k, k_a, ln_g, ln_b, sgu_w, sgu_bias)]
    out_shape = [jax.ShapeDtypeStruct((T, W), F32)] * 7 + [jax.ShapeDtypeStruct((T, GMLP_WIDTH), BF16)]
    out_specs = [pl.BlockSpec((tm, W), row)] * 7 + [pl.BlockSpec((tm, GMLP_WIDTH), row)]
    return pl.pallas_call(
        functools.partial(_mixin_kernel, tm=tm, tiles_per_seq=seq // tm),
        grid=(T // tm,),
        in_specs=in_specs,
        out_specs=out_specs,
        out_shape=out_shape,
        scratch_shapes=[pltpu.VMEM((1, D_SHIFT), F32)],
        compiler_params=pltpu.CompilerParams(
            dimension_semantics=("arbitrary",), vmem_limit_bytes=VMEM_LIMIT),
        name="mixin",
    )(x1, mix_g, w_in, mu, w0, a0, lora, g_up, k_k, k_a, ln_g, ln_b, sgu_w, sgu_bias)


def _dot_nn(a, b):
    return jnp.dot(a, b, preferred_element_type=F32)


def _dot_nt(a, b):
    return lax.dot_general(a, b, (((1,), (1,)), ((), ())), preferred_element_type=F32)


def _dot_tn(a, b):
    return lax.dot_general(a, b, (((0,), (0,)), ((), ())), preferred_element_type=F32)


def _cumsum_rows(tri3, x):
    parts = []
    for _ in range(3):
        hi = x.astype(BF16)
        parts.append(hi)
        x = x - hi.astype(F32)
    return _dot_nn(tri3, jnp.concatenate(parts, axis=0))


def _scan_kernel(r_ref, k_ref, v_ref, na_ref, bb_ref, lw_ref, g_ref, gnw_ref, gnb_ref, rk_ref,
                 y_o, s_ref, yacc_ref, *, sb, nbat):
    C = SCAN_CHUNK
    n_pairs = RWKV_WIDTH // LANES
    sls = [slice(hp * LANES, (hp + 1) * LANES) for hp in range(n_pairs)]
    streams = [(b, hp) for b in range(nbat) for hp in range(n_pairs)]
    ids = range(len(streams))

    @pl.when(pl.program_id(1) == 0)
    def _():
        s_ref[...] = jnp.zeros_like(s_ref)

    lane = lax.broadcasted_iota(jnp.int32, (1, LANES), 1)
    lane_lo = lane < RWKV_HEAD
    ci = lax.broadcasted_iota(jnp.int32, (C, C), 0)
    cj = lax.broadcasted_iota(jnp.int32, (C, C), 1)
    tri = jnp.where(ci >= cj, 1.0, 0.0).astype(BF16)
    tri3 = jnp.concatenate([tri] * 3, axis=1)
    pt = lax.broadcasted_iota(jnp.int32, (C, LANES), 0)
    ps_ = lax.broadcasted_iota(jnp.int32, (C, LANES), 1) % C
    strict = ps_ < pt
    incl = ps_ <= pt
    eye = jnp.where(ps_ == pt, 1.0, 0.0).astype(F32)
    li = lax.broadcasted_iota(jnp.int32, (LANES, LANES), 0)
    lj = lax.broadcasted_iota(jnp.int32, (LANES, LANES), 1)
    same_head = (li // RWKV_HEAD) == (lj // RWKV_HEAD)

    def stack(x):
        return jnp.concatenate([jnp.where(lane_lo, x, 0.0), jnp.where(lane_lo, 0.0, x)], axis=0).astype(BF16)

    def prep(c):
        rows = slice(c * C, (c + 1) * C)
        x2, kb, kb_s, vs, d_end = [], [], [], [], []
        for b in range(nbat):
            lw = lw_ref[b, rows, :]
            cum = _cumsum_rows(tri3, lw)
            e_pos = jnp.exp(cum)
            e_neg = jnp.exp(-cum)
            rt = r_ref[b, rows, :] * e_pos
            at = na_ref[b, rows, :] * jnp.exp(cum - lw)
            kh = k_ref[b, rows, :] * e_neg
            bh = bb_ref[b, rows, :] * e_neg
            vv = v_ref[b, rows, :]
            for sl in sls:
                x2.append(jnp.concatenate([at[:, sl], rt[:, sl]], axis=0).astype(BF16))
                kb.append(jnp.concatenate([kh[:, sl], bh[:, sl]], axis=0).astype(BF16))
                kb_s.append(jnp.concatenate([stack(kh[:, sl]), stack(bh[:, sl])], axis=0))
                vs.append((vv[:, sl].astype(BF16), stack(vv[:, sl])))
                d_end.append(e_pos[C - 1:C, sl])
        return x2, kb, kb_s, vs, d_end

    def head(ops, s):
        x2, kb, kb_s, vs, d_end = ops
        gram = [_dot_nt(x2[i], kb_s[i]) for i in ids]
        xs = [_dot_nn(x2[i], s[i].astype(BF16)) for i in ids]
        a_ab = [jnp.where(strict, gram[i][:C, LANES:], 0.0) for i in ids]
        a_kr = [jnp.concatenate([jnp.where(strict, gram[i][:C, :LANES], 0.0),
                                 jnp.where(incl, gram[i][C:, :LANES], 0.0)], axis=0).astype(BF16) for i in ids]
        a_rb = [jnp.where(incl, gram[i][C:, LANES:], 0.0).astype(BF16) for i in ids]
        av = [_dot_nn(a_kr[i], vs[i][1]) for i in ids]
        inv = [eye + a_ab[i] for i in ids]
        q = [_dot_nn(a_ab[i].astype(BF16), stack(a_ab[i])) for i in ids]
        for _ in range(int(math.log2(C)) - 2):
            t = [_dot_nn(jnp.concatenate([q[i], inv[i]], axis=0).astype(BF16), stack(q[i])) for i in ids]
            inv = [inv[i] + t[i][C:] for i in ids]
            q = [t[i][:C] for i in ids]
        inv = [inv[i] + _dot_nn(inv[i].astype(BF16), stack(q[i])) for i in ids]
        return inv, a_rb, xs, av

    def tail(c, ops, hd, s):
        x2, kb, kb_s, vs, d_end = ops
        inv, a_rb, xs, av = hd
        rows = slice(c * C, (c + 1) * C)
        u = [_dot_nn(inv[i].astype(BF16), stack(xs[i][:C] + av[i][:C])) for i in ids]
        yu = [_dot_nn(a_rb[i], stack(u[i])) for i in ids]
        upd = [_dot_tn(kb[i], jnp.concatenate([vs[i][0], u[i].astype(BF16)], axis=0))
               for i in ids]
        for i, (b, hp) in enumerate(streams):
            yacc_ref[b, rows, sls[hp]] = xs[i][C:] + av[i][C:] + yu[i]
        decay = [jnp.transpose(jnp.broadcast_to(d_end[i], (LANES, LANES))) for i in ids]
        return [(s[i] + jnp.where(same_head, upd[i], 0.0)) * decay[i] for i in ids]

    n_chunks = sb // C
    state = [s_ref[i] for i in ids]
    ops = prep(0)
    for c in range(n_chunks):
        hd = head(ops, state)
        nxt = prep(c + 1) if c + 1 < n_chunks else None
        state = tail(c, ops, hd, state)
        ops = nxt
    for i in ids:
        s_ref[i] = state[i]

    head_ones = jnp.where(same_head, 1.0, 0.0).astype(BF16)
    y = jnp.concatenate([yacc_ref[b, :, sls[hp]] for b, hp in streams], axis=0)
    hi = y.astype(BF16)
    lo = (y - hi.astype(F32)).astype(BF16)
    mean = (_dot_nn(hi, head_ones) + _dot_nn(lo, head_ones)) * (1.0 / RWKV_HEAD)
    rkk = jnp.concatenate([r_ref[b, :, sls[hp]] * k_ref[b, :, sls[hp]] * rk_ref[:, sls[hp]] for b, hp in streams],
                          axis=0)
    rks = _dot_nn(rkk.astype(BF16), head_ones)
    d = y - mean
    var = _dot_nn((d * d).astype(BF16), head_ones) * (1.0 / RWKV_HEAD)
    for i, (b, hp) in enumerate(streams):
        sl, rows = sls[hp], slice(i * sb, (i + 1) * sb)
        yn = d[rows] * lax.rsqrt(var[rows] + GN_EPS) * gnw_ref[:, sl] + gnb_ref[:, sl]
        y_o[b, :, sl] = ((yn + rks[rows] * v_ref[b, :, sl]) * g_ref[b, :, sl]).astype(BF16)


def _scan(r, k, v, na, bb, lw, g, gn_w, gn_b, r_k, *, batch, seq, sb, nbat):
    T, W = r.shape
    n_streams = nbat * (W // LANES)
    blk = lambda b, j: (b, j, 0)
    const = lambda b, j: (0, 0)
    in_specs = [pl.BlockSpec((nbat, sb, W), blk)] * 7 + [pl.BlockSpec((1, W), const)] * 3
    seq3 = lambda a: a.reshape(batch, seq, W)
    y = pl.pallas_call(
        functools.partial(_scan_kernel, sb=sb, nbat=nbat),
        grid=(batch // nbat, seq // sb),
        in_specs=in_specs,
        out_specs=pl.BlockSpec((nbat, sb, W), blk),
        out_shape=jax.ShapeDtypeStruct((batch, seq, W), BF16),
        scratch_shapes=[pltpu.VMEM((n_streams, LANES, LANES), F32), pltpu.VMEM((nbat, sb, W), F32)],
        compiler_params=pltpu.CompilerParams(
            dimension_semantics=("arbitrary", "arbitrary"), vmem_limit_bytes=VMEM_LIMIT),
        name="rwkv_scan",
    )(*(seq3(a) for a in (r, k, v, na, bb, lw, g)), gn_w, gn_b, r_k)
    return y.reshape(T, W)


def _tile(n, pref):
    return pref if n % pref == 0 else CHUNK


def kernel(x, ffn1_norm, ffn1_w1, ffn1_w3, ffn1_w2, mix_norm, w_in, mu_shift, w0, w_lora_up, a0, a_lora_up, g_lora_up, k_k, k_a, r_k, gn_w, gn_b, sgu_ln_g, sgu_ln_b, sgu_w, sgu_b, w_out, ffn2_norm, ffn2_w1, ffn2_w3, ffn2_w2, final_norm):
    B, S, D = x.shape
    T = B * S
    depth = ffn1_norm.shape[0]
    tm_ffn = _tile(S, 1024)
    tm_mix = _tile(S, 1024)
    sb = _tile(S, 256)
    rowv = lambda a: a.reshape(1, -1).astype(F32)

    xt = x.reshape(T, D)
    for l in range(depth):
        f1, f3, f2 = _narrow([ffn1_w1, ffn1_w3, ffn1_w2], l, n_steps=NARROW_STEPS)
        later = [w_in, w_out, ffn2_w1, ffn2_w3, ffn2_w2]
        xt, (w_in_b, w_out_b, w1_b, w3_b, w2_b) = _ffn(
            xt, rowv(ffn1_norm[l]), f1, f3, f2, tm=tm_ffn, narrow=later, layer=l)
        zeros = jnp.zeros((DECAY_LORA, RWKV_WIDTH), F32)
        lora = jnp.concatenate([jnp.concatenate([w_lora_up[l], zeros], axis=1),
                                jnp.concatenate([zeros, a_lora_up[l]], axis=1)], axis=0)
        sgu_bias = jnp.repeat(jnp.transpose(sgu_b[l]), GMLP_WIDTH // GMLP_GROUPS, axis=1)
        r, k, v, na, bb, lw, g, y_gmlp = _mixin(
            xt, rowv(mix_norm[l]), w_in_b, rowv(mu_shift[l]), rowv(w0[l]), rowv(a0[l]),
            lora.astype(BF16), g_lora_up[l].astype(BF16), rowv(k_k[l]), rowv(k_a[l]),
            rowv(sgu_ln_g[l]), rowv(sgu_ln_b[l]), sgu_w[l], sgu_bias, tm=tm_mix, seq=S)
        y_rwkv = _scan(r, k, v, na, bb, lw, g, rowv(gn_w[l]), rowv(gn_b[l]), rowv(r_k[l]),
                       batch=B, seq=S, sb=sb, nbat=SCAN_BATCH if B % SCAN_BATCH == 0 else 1)
        last = l == depth - 1
        xt, _ = _ffn(xt, rowv(ffn2_norm[l]), w1_b, w3_b, w2_b, tm=tm_ffn,
                     mix_in=(y_rwkv, y_gmlp, w_out_b),
                     final_g=rowv(final_norm) if last else None)
    if depth == 0:
        raise ValueError("depth must be positive")
    return xt.reshape(B, S, D)
```

```python
import functools
import math

import jax
import jax.numpy as jnp
from jax import lax
from jax.experimental import pallas as pl
from jax.experimental.pallas import tpu as pltpu

F32 = jnp.float32
BF16 = jnp.bfloat16

RWKV_WIDTH = 512
RWKV_HEAD = 64
GMLP_WIDTH = 512
GMLP_GROUPS = 8
CHUNK = 128
DECAY_LORA = 64
AAA_LORA = 64
GATE_LORA = 128
D_SHIFT = 3 * RWKV_WIDTH + DECAY_LORA + AAA_LORA + GATE_LORA
RMS_EPS = 1e-6
GN_EPS = 64e-5
LN_EPS = 1e-5
FFN_RES_SCALE = 0.5

LANES = 128
BF16_ROWS = 16
FFN_COLS = 1536
NARROW_STEPS = 4
SCAN_CHUNK = 64
SCAN_BATCH = 4
SUBLANES = 8
V7X_VMEM_BYTES = 64 * 1024 * 1024
VMEM_LIMIT = V7X_VMEM_BYTES * 7 // 8


def _rmsnorm(x, g):
    return x * lax.rsqrt(jnp.mean(x * x, axis=-1, keepdims=True) + RMS_EPS) * g


def _gelu(x):
    return 0.5 * x * (1.0 + lax.erf(x * (1.0 / math.sqrt(2.0))))


def _sigmoid(x):
    return 1.0 / (1.0 + jnp.exp(-x))


def _bdot(a, b):
    return jnp.dot(a.astype(BF16), b.astype(BF16), preferred_element_type=F32)


def _ffn_kernel(*refs, mix, final, n_cast):
    it = iter(refs)
    x_ref = next(it)
    if mix:
        yr_ref, yg_ref, wo_ref = next(it), next(it), next(it)
    g_ref, w1_ref, w3_ref, w2_ref = next(it), next(it), next(it), next(it)
    fg_ref = next(it) if final else None
    cast_in = [next(it) for _ in range(n_cast)]
    o_ref = next(it)
    cast_out = [next(it) for _ in range(n_cast)]

    x = x_ref[...]
    if mix:
        x = x + jnp.dot(yr_ref[...], wo_ref[:RWKV_WIDTH, :], preferred_element_type=F32)
        x = x + jnp.dot(yg_ref[...], wo_ref[RWKV_WIDTH:, :], preferred_element_type=F32)
    h = _rmsnorm(x, g_ref[...]).astype(BF16)
    d_ff = w1_ref.shape[1]
    acc = None
    for c0 in range(0, d_ff, FFN_COLS):
        c1 = min(c0 + FFN_COLS, d_ff)
        a = jnp.dot(h, w1_ref[:, c0:c1], preferred_element_type=F32)
        b = jnp.dot(h, w3_ref[:, c0:c1], preferred_element_type=F32)
        z = (a * _sigmoid(a) * b).astype(BF16)
        part = jnp.dot(z, w2_ref[c0:c1, :], preferred_element_type=F32)
        acc = part if acc is None else acc + part
    y = x + FFN_RES_SCALE * acc
    if final:
        y = _rmsnorm(y, fg_ref[...])
    o_ref[...] = y
    for src, dst in zip(cast_in, cast_out):
        dst[...] = src[...].astype(BF16)


def _narrow_kernel(*refs):
    half = len(refs) // 2
    for src, dst in zip(refs[:half], refs[half:]):
        dst[...] = src[...].astype(BF16)


def _row_blocks(n_rows, n_steps):
    k = 1
    while k <= n_steps:
        if (n_rows * k) % n_steps == 0 and n_steps % k == 0:
            rows = n_rows * k // n_steps
            if rows % BF16_ROWS == 0:
                return rows, k
        k *= 2
    return None


def _narrow_specs(arrs, layer, n_steps):
    specs = []
    for a in arrs:
        rows, k = _row_blocks(a.shape[1], n_steps)
        specs.append(pl.BlockSpec((None, rows, a.shape[2]), lambda i, k=k: (layer, i // k, 0)))
    return specs


def _narrow(arrs, layer, *, n_steps):
    return pl.pallas_call(
        _narrow_kernel,
        grid=(n_steps,),
        in_specs=_narrow_specs(arrs, layer, n_steps),
        out_specs=_narrow_specs(arrs, 0, n_steps),
        out_shape=[jax.ShapeDtypeStruct((1,) + a.shape[1:], BF16) for a in arrs],
        compiler_params=pltpu.CompilerParams(dimension_semantics=("arbitrary",), vmem_limit_bytes=VMEM_LIMIT),
        name="narrow_weights",
    )(*arrs)


def _whole(a, layered=False):
    if layered:
        return pl.BlockSpec((None,) + a.shape[1:], lambda i: (0, 0, 0), pipeline_mode=pl.Buffered(1))
    return pl.BlockSpec(a.shape, lambda i: (0,) * a.ndim, pipeline_mode=pl.Buffered(1))


def _ffn(x, norm_g, w1, w3, w2, *, tm, mix_in=None, final_g=None, narrow=(), layer=0):
    T, D = x.shape
    n = T // tm
    mix = mix_in is not None
    final = final_g is not None
    row = lambda i: (i, 0)
    whole = lambda a: _whole(a, layered=a.ndim == 3)
    in_specs = [pl.BlockSpec((tm, D), row)]
    args = [x]
    if mix:
        yr, yg, wo = mix_in
        in_specs += [pl.BlockSpec((tm, yr.shape[1]), row), pl.BlockSpec((tm, yg.shape[1]), row), whole(wo)]
        args += [yr, yg, wo]
    in_specs += [whole(norm_g), whole(w1), whole(w3), whole(w2)]
    args += [norm_g, w1, w3, w2]
    if final:
        in_specs.append(whole(final_g))
        args.append(final_g)
    out_specs = [pl.BlockSpec((tm, D), row)]
    out_shape = [jax.ShapeDtypeStruct((T, D), F32)]
    in_specs += _narrow_specs(narrow, layer, n)
    args += list(narrow)
    out_specs += _narrow_specs(narrow, 0, n)
    out_shape += [jax.ShapeDtypeStruct((1,) + a.shape[1:], BF16) for a in narrow]
    outs = pl.pallas_call(
        functools.partial(_ffn_kernel, mix=mix, final=final, n_cast=len(narrow)),
        grid=(n,),
        in_specs=in_specs,
        out_specs=out_specs,
        out_shape=out_shape,
        compiler_params=pltpu.CompilerParams(
            dimension_semantics=("arbitrary",), vmem_limit_bytes=VMEM_LIMIT),
        name="ffn_mix_final" if mix else "ffn",
    )(*args)
    return outs[0], list(outs[1:])


def _head_sums(x, lane_lo):
    s0 = jnp.sum(jnp.where(lane_lo, x, 0.0), axis=-1, keepdims=True)
    s1 = jnp.sum(jnp.where(lane_lo, 0.0, x), axis=-1, keepdims=True)
    return jnp.where(lane_lo, s0, s1)


def _mixin_kernel(x_ref, g_ref, win_ref, mu_ref, w0_ref, a0_ref, lora_ref, gup_ref, kk_ref, ka_ref,
                  lng_ref, lnb_ref, sw_ref, sb_ref,
                  r_o, k_o, v_o, na_o, bb_o, lw_o, g_o, ygm_o, prev_ref, *, tm, tiles_per_seq):
    i = pl.program_id(0)
    W = RWKV_WIDTH
    h = _rmsnorm(x_ref[...], g_ref[...]).astype(BF16)
    p = jnp.dot(h, win_ref[...], preferred_element_type=F32)

    ps = p[:, :D_SHIFT]
    rolled = pltpu.roll(ps, 1, 0)
    first = (i % tiles_per_seq) == 0
    prev = jnp.where(first, 0.0, prev_ref[...])
    row = lax.broadcasted_iota(jnp.int32, (SUBLANES, 1), 0)
    shifted = jnp.concatenate([jnp.where(row == 0, prev, rolled[:SUBLANES]), rolled[SUBLANES:]], axis=0)
    prev_ref[...] = ps[tm - 1:tm, :]
    ps = ps + (shifted - ps) * mu_ref[...]

    r = ps[:, :W]
    k = ps[:, W:2 * W]
    v = ps[:, 2 * W:3 * W]
    slab = ps[:, 3 * W:3 * W + DECAY_LORA + AAA_LORA]
    pg = ps[:, 3 * W + DECAY_LORA + AAA_LORA:]
    lane = lax.broadcasted_iota(jnp.int32, (1, LANES), 1)
    slab = jnp.where(lane < DECAY_LORA, jnp.tanh(slab), slab)
    lora = _bdot(slab, lora_ref[...])
    lw_o[...] = -math.exp(-0.5) * _sigmoid(w0_ref[...] + lora[:, :W])
    a = _sigmoid(a0_ref[...] + lora[:, W:])
    g_o[...] = _bdot(_sigmoid(pg), gup_ref[...])

    lane_lo = lane < RWKV_HEAD
    kk = k * kk_ref[...]
    for hp in range(W // LANES):
        sl = slice(hp * LANES, (hp + 1) * LANES)
        kkp = kk[:, sl]
        ss = _head_sums(kkp * kkp, lane_lo)
        kkn = kkp * jnp.minimum(lax.rsqrt(ss), 1e12)
        na_o[:, sl] = -kkn
        bb_o[:, sl] = kkn * a[:, sl]
    r_o[...] = r
    k_o[...] = k * (1.0 + (a - 1.0) * ka_ref[...])
    v_o[...] = v

    u = _gelu(p[:, D_SHIFT:D_SHIFT + GMLP_WIDTH])
    vg = _gelu(p[:, D_SHIFT + GMLP_WIDTH:])
    mean = jnp.mean(vg, axis=-1, keepdims=True)
    d = vg - mean
    var = jnp.mean(d * d, axis=-1, keepdims=True)
    vn = (d * lax.rsqrt(var + LN_EPS) * lng_ref[...] + lnb_ref[...]).astype(BF16)
    tr = lax.broadcasted_iota(jnp.int32, (CHUNK, CHUNK), 0)
    tc = lax.broadcasted_iota(jnp.int32, (CHUNK, CHUNK), 1)
    ws = [jnp.where(tr >= tc, sw_ref[gi], 0.0).astype(BF16) for gi in range(GMLP_GROUPS)]
    bias = sb_ref[...]
    for c in range(tm // CHUNK):
        rows = slice(c * CHUNK, (c + 1) * CHUNK)
        outs = []
        for hp in range(GMLP_WIDTH // LANES):
            vp = vn[rows, hp * LANES:(hp + 1) * LANES]
            m0 = jnp.dot(ws[2 * hp], vp, preferred_element_type=F32)
            m1 = jnp.dot(ws[2 * hp + 1], vp, preferred_element_type=F32)
            outs.append(jnp.where(lane_lo, m0, m1))
        mixed = jnp.concatenate(outs, axis=1) + bias
        ygm_o[rows, :] = (u[rows, :] * mixed).astype(BF16)


def _mixin(x1, mix_g, w_in, mu, w0, a0, lora, g_up, k_k, k_a, ln_g, ln_b, sgu_w, sgu_bias, *, tm, seq):
    T, D = x1.shape
    W = RWKV_WIDTH
    row = lambda i: (i, 0)
    const = lambda i: (0, 0)
    in_specs = [pl.BlockSpec((tm, D), row)] + [_whole(a, layered=a is w_in) for a in
                (mix_g, w_in, mu, w0, a0, lora, g_up, k_k, k_a, ln_g, ln_b, sgu_w, sgu_bias)]
    out_shape = [jax.ShapeDtypeStruct((T, W), F32)] * 7 + [jax.ShapeDtypeStruct((T, GMLP_WIDTH), BF16)]
    out_specs = [pl.BlockSpec((tm, W), row)] * 7 + [pl.BlockSpec((tm, GMLP_WIDTH), row)]
    return pl.pallas_call(
        functools.partial(_mixin_kernel, tm=tm, tiles_per_seq=seq // tm),
        grid=(T // tm,),
        in_specs=in_specs,
        out_specs=out_specs,
        out_shape=out_shape,
        scratch_shapes=[pltpu.VMEM((1, D_SHIFT), F32)],
        compiler_params=pltpu.CompilerParams(
            dimension_semantics=("arbitrary",), vmem_limit_bytes=VMEM_LIMIT),
        name="mixin",
    )(x1, mix_g, w_in, mu, w0, a0, lora, g_up, k_k, k_a, ln_g, ln_b, sgu_w, sgu_bias)


def _dot_nn(a, b):
    return jnp.dot(a, b, preferred_element_type=F32)


def _dot_nt(a, b):
    return lax.dot_general(a, b, (((1,), (1,)), ((), ())), preferred_element_type=F32)


def _dot_tn(a, b):
    return lax.dot_general(a, b, (((0,), (0,)), ((), ())), preferred_element_type=F32)


def _cumsum_rows(tri, x):
    out = None
    for _ in range(3):
        hi = x.astype(BF16)
        t = _dot_nn(tri, hi)
        out = t if out is None else out + t
        x = x - hi.astype(F32)
    return out


def _scan_kernel(r_ref, k_ref, v_ref, na_ref, bb_ref, lw_ref, g_ref, gnw_ref, gnb_ref, rk_ref,
                 y_o, s_ref, yacc_ref, *, sb, nbat):
    C = SCAN_CHUNK
    n_pairs = RWKV_WIDTH // LANES
    sls = [slice(hp * LANES, (hp + 1) * LANES) for hp in range(n_pairs)]
    streams = [(b, hp) for b in range(nbat) for hp in range(n_pairs)]
    ids = range(len(streams))

    @pl.when(pl.program_id(1) == 0)
    def _():
        s_ref[...] = jnp.zeros_like(s_ref)

    lane = lax.broadcasted_iota(jnp.int32, (1, LANES), 1)
    lane_lo = lane < RWKV_HEAD
    ci = lax.broadcasted_iota(jnp.int32, (C, C), 0)
    cj = lax.broadcasted_iota(jnp.int32, (C, C), 1)
    tri = jnp.where(ci >= cj, 1.0, 0.0).astype(BF16)
    pt = lax.broadcasted_iota(jnp.int32, (C, LANES), 0)
    ps_ = lax.broadcasted_iota(jnp.int32, (C, LANES), 1) % C
    strict = ps_ < pt
    incl = ps_ <= pt
    eye = jnp.where(ps_ == pt, 1.0, 0.0).astype(F32)
    li = lax.broadcasted_iota(jnp.int32, (LANES, LANES), 0)
    lj = lax.broadcasted_iota(jnp.int32, (LANES, LANES), 1)
    same_head = (li // RWKV_HEAD) == (lj // RWKV_HEAD)

    def stack(x):
        return jnp.concatenate([jnp.where(lane_lo, x, 0.0), jnp.where(lane_lo, 0.0, x)], axis=0).astype(BF16)

    def prep(c):
        rows = slice(c * C, (c + 1) * C)
        x2, kb, kb_s, vs, d_end = [], [], [], [], []
        for b in range(nbat):
            lw = lw_ref[b, rows, :]
            cum = _cumsum_rows(tri, lw)
            e_pos = jnp.exp(cum)
            e_neg = jnp.exp(-cum)
            rt = r_ref[b, rows, :] * e_pos
            at = na_ref[b, rows, :] * jnp.exp(cum - lw)
            kh = k_ref[b, rows, :] * e_neg
            bh = bb_ref[b, rows, :] * e_neg
            vv = v_ref[b, rows, :]
            for sl in sls:
                x2.append(jnp.concatenate([at[:, sl], rt[:, sl]], axis=0).astype(BF16))
                kb.append(jnp.concatenate([kh[:, sl], bh[:, sl]], axis=0).astype(BF16))
                kb_s.append(jnp.concatenate([stack(kh[:, sl]), stack(bh[:, sl])], axis=0))
                vs.append((vv[:, sl].astype(BF16), stack(vv[:, sl])))
                d_end.append(e_pos[C - 1:C, sl])
        return x2, kb, kb_s, vs, d_end

    def head(ops, s):
        x2, kb, kb_s, vs, d_end = ops
        gram = [_dot_nt(x2[i], kb_s[i]) for i in ids]
        xs = [_dot_nn(x2[i], s[i].astype(BF16)) for i in ids]
        a_ab = [jnp.where(strict, gram[i][:C, LANES:], 0.0) for i in ids]
        a_kr = [jnp.concatenate([jnp.where(strict, gram[i][:C, :LANES], 0.0),
                                 jnp.where(incl, gram[i][C:, :LANES], 0.0)], axis=0).astype(BF16) for i in ids]
        a_rb = [jnp.where(incl, gram[i][C:, LANES:], 0.0).astype(BF16) for i in ids]
        av = [_dot_nn(a_kr[i], vs[i][1]) for i in ids]
        inv = [eye + a_ab[i] for i in ids]
        q = [_dot_nn(a_ab[i].astype(BF16), stack(a_ab[i])) for i in ids]
        for _ in range(int(math.log2(C)) - 2):
            t = [_dot_nn(jnp.concatenate([q[i], inv[i]], axis=0).astype(BF16), stack(q[i])) for i in ids]
            inv = [inv[i] + t[i][C:] for i in ids]
            q = [t[i][:C] for i in ids]
        inv = [inv[i] + _dot_nn(inv[i].astype(BF16), stack(q[i])) for i in ids]
        return inv, a_rb, xs, av

    def tail(c, ops, hd, s):
        x2, kb, kb_s, vs, d_end = ops
        inv, a_rb, xs, av = hd
        rows = slice(c * C, (c + 1) * C)
        u = [_dot_nn(inv[i].astype(BF16), stack(xs[i][:C] + av[i][:C])) for i in ids]
        yu = [_dot_nn(a_rb[i], stack(u[i])) for i in ids]
        upd = [_dot_tn(kb[i], jnp.concatenate([vs[i][0], u[i].astype(BF16)], axis=0))
               for i in ids]
        for i, (b, hp) in enumerate(streams):
            yacc_ref[b, rows, sls[hp]] = xs[i][C:] + av[i][C:] + yu[i]
        decay = [jnp.transpose(jnp.broadcast_to(d_end[i], (LANES, LANES))) for i in ids]
        return [(s[i] + jnp.where(same_head, upd[i], 0.0)) * decay[i] for i in ids]

    n_chunks = sb // C
    state = [s_ref[i] for i in ids]
    ops = prep(0)
    for c in range(n_chunks):
        hd = head(ops, state)
        nxt = prep(c + 1) if c + 1 < n_chunks else None
        state = tail(c, ops, hd, state)
        ops = nxt
    for i in ids:
        s_ref[i] = state[i]

    head_ones = jnp.where(same_head, 1.0, 0.0).astype(BF16)
    y = jnp.concatenate([yacc_ref[b, :, sls[hp]] for b, hp in streams], axis=0)
    hi = y.astype(BF16)
    lo = (y - hi.astype(F32)).astype(BF16)
    mean = (_dot_nn(hi, head_ones) + _dot_nn(lo, head_ones)) * (1.0 / RWKV_HEAD)
    rkk = jnp.concatenate([r_ref[b, :, sls[hp]] * k_ref[b, :, sls[hp]] * rk_ref[:, sls[hp]] for b, hp in streams],
                          axis=0)
    rks = _dot_nn(rkk.astype(BF16), head_ones)
    d = y - mean
    var = _dot_nn((d * d).astype(BF16), head_ones) * (1.0 / RWKV_HEAD)
    for i, (b, hp) in enumerate(streams):
        sl, rows = sls[hp], slice(i * sb, (i + 1) * sb)
        yn = d[rows] * lax.rsqrt(var[rows] + GN_EPS) * gnw_ref[:, sl] + gnb_ref[:, sl]
        y_o[b, :, sl] = ((yn + rks[rows] * v_ref[b, :, sl]) * g_ref[b, :, sl]).astype(BF16)


def _scan(r, k, v, na, bb, lw, g, gn_w, gn_b, r_k, *, batch, seq, sb, nbat):
    T, W = r.shape
    n_streams = nbat * (W // LANES)
    blk = lambda b, j: (b, j, 0)
    const = lambda b, j: (0, 0)
    in_specs = [pl.BlockSpec((nbat, sb, W), blk)] * 7 + [pl.BlockSpec((1, W), const)] * 3
    seq3 = lambda a: a.reshape(batch, seq, W)
    y = pl.pallas_call(
        functools.partial(_scan_kernel, sb=sb, nbat=nbat),
        grid=(batch // nbat, seq // sb),
        in_specs=in_specs,
        out_specs=pl.BlockSpec((nbat, sb, W), blk),
        out_shape=jax.ShapeDtypeStruct((batch, seq, W), BF16),
        scratch_shapes=[pltpu.VMEM((n_streams, LANES, LANES), F32), pltpu.VMEM((nbat, sb, W), F32)],
        compiler_params=pltpu.CompilerParams(
            dimension_semantics=("arbitrary", "arbitrary"), vmem_limit_bytes=VMEM_LIMIT),
        name="rwkv_scan",
    )(*(seq3(a) for a in (r, k, v, na, bb, lw, g)), gn_w, gn_b, r_k)
    return y.reshape(T, W)


def _tile(n, pref):
    return pref if n % pref == 0 else CHUNK


def kernel(x, ffn1_norm, ffn1_w1, ffn1_w3, ffn1_w2, mix_norm, w_in, mu_shift, w0, w_lora_up, a0, a_lora_up, g_lora_up, k_k, k_a, r_k, gn_w, gn_b, sgu_ln_g, sgu_ln_b, sgu_w, sgu_b, w_out, ffn2_norm, ffn2_w1, ffn2_w3, ffn2_w2, final_norm):
    B, S, D = x.shape
    T = B * S
    depth = ffn1_norm.shape[0]
    tm_ffn = _tile(S, 1024)
    tm_mix = _tile(S, 1024)
    sb = _tile(S, 256)
    rowv = lambda a: a.reshape(1, -1).astype(F32)

    xt = x.reshape(T, D)
    for l in range(depth):
        f1, f3, f2 = _narrow([ffn1_w1, ffn1_w3, ffn1_w2], l, n_steps=NARROW_STEPS)
        later = [w_in, w_out, ffn2_w1, ffn2_w3, ffn2_w2]
        xt, (w_in_b, w_out_b, w1_b, w3_b, w2_b) = _ffn(
            xt, rowv(ffn1_norm[l]), f1, f3, f2, tm=tm_ffn, narrow=later, layer=l)
        zeros = jnp.zeros((DECAY_LORA, RWKV_WIDTH), F32)
        lora = jnp.concatenate([jnp.concatenate([w_lora_up[l], zeros], axis=1),
                                jnp.concatenate([zeros, a_lora_up[l]], axis=1)], axis=0)
        sgu_bias = jnp.repeat(jnp.transpose(sgu_b[l]), GMLP_WIDTH // GMLP_GROUPS, axis=1)
        r, k, v, na, bb, lw, g, y_gmlp = _mixin(
            xt, rowv(mix_norm[l]), w_in_b, rowv(mu_shift[l]), rowv(w0[l]), rowv(a0[l]),
            lora.astype(BF16), g_lora_up[l].astype(BF16), rowv(k_k[l]), rowv(k_a[l]),
            rowv(sgu_ln_g[l]), rowv(sgu_ln_b[l]), sgu_w[l], sgu_bias, tm=tm_mix, seq=S)
        y_rwkv = _scan(r, k, v, na, bb, lw, g, rowv(gn_w[l]), rowv(gn_b[l]), rowv(r_k[l]),
                       batch=B, seq=S, sb=sb, nbat=SCAN_BATCH if B % SCAN_BATCH == 0 else 1)
        last = l == depth - 1
        xt, _ = _ffn(xt, rowv(ffn2_norm[l]), w1_b, w3_b, w2_b, tm=tm_ffn,
                     mix_in=(y_rwkv, y_gmlp, w_out_b),
                     final_g=rowv(final_norm) if last else None)
    if depth == 0:
        raise ValueError("depth must be positive")
    return xt.reshape(B, S, D)
```

```python
import functools
import math

import jax
import jax.numpy as jnp
from jax import lax
from jax.experimental import pallas as pl
from jax.experimental.pallas import tpu as pltpu

F32 = jnp.float32
BF16 = jnp.bfloat16

RWKV_WIDTH = 512
RWKV_HEAD = 64
GMLP_WIDTH = 512
GMLP_GROUPS = 8
CHUNK = 128
DECAY_LORA = 64
AAA_LORA = 64
GATE_LORA = 128
D_SHIFT = 3 * RWKV_WIDTH + DECAY_LORA + AAA_LORA + GATE_LORA
RMS_EPS = 1e-6
GN_EPS = 64e-5
LN_EPS = 1e-5
FFN_RES_SCALE = 0.5

LANES = 128
BF16_ROWS = 16
FFN_COLS = 1536
NARROW_STEPS = 4
SCAN_CHUNK = 64
SCAN_BATCH = 4
SUBLANES = 8
V7X_VMEM_BYTES = 64 * 1024 * 1024
VMEM_LIMIT = V7X_VMEM_BYTES * 7 // 8


def _rmsnorm(x, g):
    return x * lax.rsqrt(jnp.mean(x * x, axis=-1, keepdims=True) + RMS_EPS) * g


def _gelu(x):
    return 0.5 * x * (1.0 + lax.erf(x * (1.0 / math.sqrt(2.0))))


def _sigmoid(x):
    return 1.0 / (1.0 + jnp.exp(-x))


def _bdot(a, b):
    return jnp.dot(a.astype(BF16), b.astype(BF16), preferred_element_type=F32)


def _ffn_kernel(*refs, mix, final, n_cast):
    it = iter(refs)
    x_ref = next(it)
    if mix:
        yr_ref, yg_ref, wo_ref = next(it), next(it), next(it)
    g_ref, w1_ref, w3_ref, w2_ref = next(it), next(it), next(it), next(it)
    fg_ref = next(it) if final else None
    cast_in = [next(it) for _ in range(n_cast)]
    o_ref = next(it)
    cast_out = [next(it) for _ in range(n_cast)]

    x = x_ref[...]
    if mix:
        x = x + jnp.dot(yr_ref[...], wo_ref[:RWKV_WIDTH, :], preferred_element_type=F32)
        x = x + jnp.dot(yg_ref[...], wo_ref[RWKV_WIDTH:, :], preferred_element_type=F32)
    h = _rmsnorm(x, g_ref[...]).astype(BF16)
    d_ff = w1_ref.shape[1]
    acc = None
    for c0 in range(0, d_ff, FFN_COLS):
        c1 = min(c0 + FFN_COLS, d_ff)
        a = jnp.dot(h, w1_ref[:, c0:c1], preferred_element_type=F32)
        b = jnp.dot(h, w3_ref[:, c0:c1], preferred_element_type=F32)
        z = (a * _sigmoid(a) * b).astype(BF16)
        part = jnp.dot(z, w2_ref[c0:c1, :], preferred_element_type=F32)
        acc = part if acc is None else acc + part
    y = x + FFN_RES_SCALE * acc
    if final:
        y = _rmsnorm(y, fg_ref[...])
    o_ref[...] = y
    for src, dst in zip(cast_in, cast_out):
        dst[...] = src[...].astype(BF16)


def _narrow_kernel(*refs):
    half = len(refs) // 2
    for src, dst in zip(refs[:half], refs[half:]):
        dst[...] = src[...].astype(BF16)


def _row_blocks(n_rows, n_steps):
    k = 1
    while k <= n_steps:
        if (n_rows * k) % n_steps == 0 and n_steps % k == 0:
            rows = n_rows * k // n_steps
            if rows % BF16_ROWS == 0:
                return rows, k
        k *= 2
    return None


def _narrow_specs(arrs, layer, n_steps):
    specs = []
    for a in arrs:
        rows, k = _row_blocks(a.shape[1], n_steps)
        specs.append(pl.BlockSpec((None, rows, a.shape[2]), lambda i, k=k: (layer, i // k, 0)))
    return specs


def _narrow(arrs, layer, *, n_steps):
    return pl.pallas_call(
        _narrow_kernel,
        grid=(n_steps,),
        in_specs=_narrow_specs(arrs, layer, n_steps),
        out_specs=_narrow_specs(arrs, 0, n_steps),
        out_shape=[jax.ShapeDtypeStruct((1,) + a.shape[1:], BF16) for a in arrs],
        compiler_params=pltpu.CompilerParams(dimension_semantics=("arbitrary",), vmem_limit_bytes=VMEM_LIMIT),
        name="narrow_weights",
    )(*arrs)


def _whole(a, layered=False):
    if layered:
        return pl.BlockSpec((None,) + a.shape[1:], lambda i: (0, 0, 0), pipeline_mode=pl.Buffered(1))
    return pl.BlockSpec(a.shape, lambda i: (0,) * a.ndim, pipeline_mode=pl.Buffered(1))


def _ffn(x, norm_g, w1, w3, w2, *, tm, mix_in=None, final_g=None, narrow=(), layer=0):
    T, D = x.shape
    n = T // tm
    mix = mix_in is not None
    final = final_g is not None
    row = lambda i: (i, 0)
    whole = lambda a: _whole(a, layered=a.ndim == 3)
    in_specs = [pl.BlockSpec((tm, D), row)]
    args = [x]
    if mix:
        yr, yg, wo = mix_in
        in_specs += [pl.BlockSpec((tm, yr.shape[1]), row), pl.BlockSpec((tm, yg.shape[1]), row), whole(wo)]
        args += [yr, yg, wo]
    in_specs += [whole(norm_g), whole(w1), whole(w3), whole(w2)]
    args += [norm_g, w1, w3, w2]
    if final:
        in_specs.append(whole(final_g))
        args.append(final_g)
    out_specs = [pl.BlockSpec((tm, D), row)]
    out_shape = [jax.ShapeDtypeStruct((T, D), F32)]
    in_specs += _narrow_specs(narrow, layer, n)
    args += list(narrow)
    out_specs += _narrow_specs(narrow, 0, n)
    out_shape += [jax.ShapeDtypeStruct((1,) + a.shape[1:], BF16) for a in narrow]
    outs = pl.pallas_call(
        functools.partial(_ffn_kernel, mix=mix, final=final, n_cast=len(narrow)),
        grid=(n,),
        in_specs=in_specs,
        out_specs=out_specs,
        out_shape=out_shape,
        compiler_params=pltpu.CompilerParams(
            dimension_semantics=("arbitrary",), vmem_limit_bytes=VMEM_LIMIT),
        name="ffn_mix_final" if mix else "ffn",
    )(*args)
    return outs[0], list(outs[1:])


def _head_sums(x, lane_lo):
    s0 = jnp.sum(jnp.where(lane_lo, x, 0.0), axis=-1, keepdims=True)
    s1 = jnp.sum(jnp.where(lane_lo, 0.0, x), axis=-1, keepdims=True)
    return jnp.where(lane_lo, s0, s1)


def _mixin_kernel(x_ref, g_ref, win_ref, mu_ref, w0_ref, a0_ref, lora_ref, gup_ref, kk_ref, ka_ref,
                  lng_ref, lnb_ref, sw_ref, sb_ref,
                  r_o, k_o, v_o, na_o, bb_o, lw_o, g_o, ygm_o, prev_ref, *, tm, tiles_per_seq):
    i = pl.program_id(0)
    W = RWKV_WIDTH
    h = _rmsnorm(x_ref[...], g_ref[...]).astype(BF16)
    p = jnp.dot(h, win_ref[...], preferred_element_type=F32)

    ps = p[:, :D_SHIFT]
    rolled = pltpu.roll(ps, 1, 0)
    first = (i % tiles_per_seq) == 0
    prev = jnp.where(first, 0.0, prev_ref[...])
    row = lax.broadcasted_iota(jnp.int32, (SUBLANES, 1), 0)
    shifted = jnp.concatenate([jnp.where(row == 0, prev, rolled[:SUBLANES]), rolled[SUBLANES:]], axis=0)
    prev_ref[...] = ps[tm - 1:tm, :]
    ps = ps + (shifted - ps) * mu_ref[...]

    r = ps[:, :W]
    k = ps[:, W:2 * W]
    v = ps[:, 2 * W:3 * W]
    slab = ps[:, 3 * W:3 * W + DECAY_LORA + AAA_LORA]
    pg = ps[:, 3 * W + DECAY_LORA + AAA_LORA:]
    lane = lax.broadcasted_iota(jnp.int32, (1, LANES), 1)
    slab = jnp.where(lane < DECAY_LORA, jnp.tanh(slab), slab)
    lora = _bdot(slab, lora_ref[...])
    lw_o[...] = -math.exp(-0.5) * _sigmoid(w0_ref[...] + lora[:, :W])
    a = _sigmoid(a0_ref[...] + lora[:, W:])
    g_o[...] = _bdot(_sigmoid(pg), gup_ref[...])

    lane_lo = lane < RWKV_HEAD
    kk = k * kk_ref[...]
    for hp in range(W // LANES):
        sl = slice(hp * LANES, (hp + 1) * LANES)
        kkp = kk[:, sl]
        ss = _head_sums(kkp * kkp, lane_lo)
        kkn = kkp * jnp.minimum(lax.rsqrt(ss), 1e12)
        na_o[:, sl] = -kkn
        bb_o[:, sl] = kkn * a[:, sl]
    r_o[...] = r
    k_o[...] = k * (1.0 + (a - 1.0) * ka_ref[...])
    v_o[...] = v

    u = _gelu(p[:, D_SHIFT:D_SHIFT + GMLP_WIDTH])
    vg = _gelu(p[:, D_SHIFT + GMLP_WIDTH:])
    mean = jnp.mean(vg, axis=-1, keepdims=True)
    d = vg - mean
    var = jnp.mean(d * d, axis=-1, keepdims=True)
    vn = (d * lax.rsqrt(var + LN_EPS) * lng_ref[...] + lnb_ref[...]).astype(BF16)
    tr = lax.broadcasted_iota(jnp.int32, (CHUNK, CHUNK), 0)
    tc = lax.broadcasted_iota(jnp.int32, (CHUNK, CHUNK), 1)
    ws = [jnp.where(tr >= tc, sw_ref[gi], 0.0).astype(BF16) for gi in range(GMLP_GROUPS)]
    bias = sb_ref[...]
    for c in range(tm // CHUNK):
        rows = slice(c * CHUNK, (c + 1) * CHUNK)
        outs = []
        for hp in range(GMLP_WIDTH // LANES):
            vp = vn[rows, hp * LANES:(hp + 1) * LANES]
            m0 = jnp.dot(ws[2 * hp], vp, preferred_element_type=F32)
            m1 = jnp.dot(ws[2 * hp + 1], vp, preferred_element_type=F32)
            outs.append(jnp.where(lane_lo, m0, m1))
        mixed = jnp.concatenate(outs, axis=1) + bias
        ygm_o[rows, :] = (u[rows, :] * mixed).astype(BF16)


def _mixin(x1, mix_g, w_in, mu, w0, a0, lora, g_up, k_k, k_a, ln_g, ln_b, sgu_w, sgu_bias, *, tm, seq):
    T, D = x1.shape
    W = RWKV_WIDTH
    row = lambda i: (i, 0)
    const = lambda i: (0, 0)
    in_specs = [pl.BlockSpec((tm, D), row)] + [_whole(a, layered=a is w_in) for a in
                (mix_g, w_in, mu, w0, a0, lora, g_up, k_k, k_a, ln_g, ln_b, sgu_w, sgu_bias)]
    out_shape = [jax.ShapeDtypeStruct((T, W), F32)] * 7 + [jax.ShapeDtypeStruct((T, GMLP_WIDTH), BF16)]
    out_specs = [pl.BlockSpec((tm, W), row)] * 7 + [pl.BlockSpec((tm, GMLP_WIDTH), row)]
    return pl.pallas_call(
        functools.partial(_mixin_kernel, tm=tm, tiles_per_seq=seq // tm),
        grid=(T // tm,),
        in_specs=in_specs,
        out_specs=out_specs,
        out_shape=out_shape,
        scratch_shapes=[pltpu.VMEM((1, D_SHIFT), F32)],
        compiler_params=pltpu.CompilerParams(
            dimension_semantics=("arbitrary",), vmem_limit_bytes=VMEM_LIMIT),
        name="mixin",
    )(x1, mix_g, w_in, mu, w0, a0, lora, g_up, k_k, k_a, ln_g, ln_b, sgu_w, sgu_bias)


def _dot_nn(a, b):
    return jnp.dot(a, b, preferred_element_type=F32)


def _dot_nt(a, b):
    return lax.dot_general(a, b, (((1,), (1,)), ((), ())), preferred_element_type=F32)


def _dot_tn(a, b):
    return lax.dot_general(a, b, (((0,), (0,)), ((), ())), preferred_element_type=F32)


def _cumsum_rows(tri, x):
    out = None
    for _ in range(3):
        hi = x.astype(BF16)
        t = _dot_nn(tri, hi)
        out = t if out is None else out + t
        x = x - hi.astype(F32)
    return out


def _scan_kernel(r_ref, k_ref, v_ref, na_ref, bb_ref, lw_ref, g_ref, gnw_ref, gnb_ref, rk_ref,
                 y_o, s_ref, yacc_ref, *, sb, nbat):
    C = SCAN_CHUNK
    n_pairs = RWKV_WIDTH // LANES
    sls = [slice(hp * LANES, (hp + 1) * LANES) for hp in range(n_pairs)]
    streams = [(b, hp) for b in range(nbat) for hp in range(n_pairs)]
    ids = range(len(streams))

    @pl.when(pl.program_id(1) == 0)
    def _():
        s_ref[...] = jnp.zeros_like(s_ref)

    lane = lax.broadcasted_iota(jnp.int32, (1, LANES), 1)
    lane_lo = lane < RWKV_HEAD
    ci = lax.broadcasted_iota(jnp.int32, (C, C), 0)
    cj = lax.broadcasted_iota(jnp.int32, (C, C), 1)
    tri = jnp.where(ci >= cj, 1.0, 0.0).astype(BF16)
    pt = lax.broadcasted_iota(jnp.int32, (C, LANES), 0)
    ps_ = lax.broadcasted_iota(jnp.int32, (C, LANES), 1) % C
    strict = ps_ < pt
    incl = ps_ <= pt
    eye = jnp.where(ps_ == pt, 1.0, 0.0).astype(F32)
    li = lax.broadcasted_iota(jnp.int32, (LANES, LANES), 0)
    lj = lax.broadcasted_iota(jnp.int32, (LANES, LANES), 1)
    same_head = (li // RWKV_HEAD) == (lj // RWKV_HEAD)

    def stack(x):
        return jnp.concatenate([jnp.where(lane_lo, x, 0.0), jnp.where(lane_lo, 0.0, x)], axis=0).astype(BF16)

    def prep(c):
        rows = slice(c * C, (c + 1) * C)
        x2, kb, kb_s, vs, d_end = [], [], [], [], []
        for b in range(nbat):
            lw = lw_ref[b, rows, :]
            cum = _cumsum_rows(tri, lw)
            e_pos = jnp.exp(cum)
            e_neg = jnp.exp(-cum)
            rt = r_ref[b, rows, :] * e_pos
            at = na_ref[b, rows, :] * jnp.exp(cum - lw)
            kh = k_ref[b, rows, :] * e_neg
            bh = bb_ref[b, rows, :] * e_neg
            vv = v_ref[b, rows, :]
            for sl in sls:
                x2.append(jnp.concatenate([at[:, sl], rt[:, sl]], axis=0).astype(BF16))
                kb.append(jnp.concatenate([kh[:, sl], bh[:, sl]], axis=0).astype(BF16))
                kb_s.append(jnp.concatenate([stack(kh[:, sl]), stack(bh[:, sl])], axis=0))
                vs.append((vv[:, sl].astype(BF16), stack(vv[:, sl])))
                d_end.append(e_pos[C - 1:C, sl])
        return x2, kb, kb_s, vs, d_end

    def head(ops, s):
        x2, kb, kb_s, vs, d_end = ops
        gram = [_dot_nt(x2[i], kb_s[i]) for i in ids]
        xs = [_dot_nn(x2[i], s[i].astype(BF16)) for i in ids]
        a_ab = [jnp.where(strict, gram[i][:C, LANES:], 0.0) for i in ids]
        a_kr = [jnp.concatenate([jnp.where(strict, gram[i][:C, :LANES], 0.0),
                                 jnp.where(incl, gram[i][C:, :LANES], 0.0)], axis=0).astype(BF16) for i in ids]
        a_rb = [jnp.where(incl, gram[i][C:, LANES:], 0.0).astype(BF16) for i in ids]
        av = [_dot_nn(a_kr[i], vs[i][1]) for i in ids]
        inv = [eye + a_ab[i] for i in ids]
        q = [_dot_nn(a_ab[i].astype(BF16), stack(a_ab[i])) for i in ids]
        for _ in range(int(math.log2(C)) - 2):
            t = [_dot_nn(jnp.concatenate([q[i], inv[i]], axis=0).astype(BF16), stack(q[i])) for i in ids]
            inv = [inv[i] + t[i][C:] for i in ids]
            q = [t[i][:C] for i in ids]
        inv = [inv[i] + _dot_nn(inv[i].astype(BF16), stack(q[i])) for i in ids]
        inv2 = [jnp.concatenate([inv[i], _dot_nn(a_rb[i], stack(inv[i]))], axis=0).astype(BF16) for i in ids]
        return inv2, a_rb, xs, av

    def tail(c, ops, hd, s):
        x2, kb, kb_s, vs, d_end = ops
        inv2, a_rb, xs, av = hd
        rows = slice(c * C, (c + 1) * C)
        uy = [_dot_nn(inv2[i], stack(xs[i][:C] + av[i][:C])) for i in ids]
        u = [uy[i][:C] for i in ids]
        yu = [uy[i][C:] for i in ids]
        upd = [_dot_tn(kb[i], jnp.concatenate([vs[i][0], u[i].astype(BF16)], axis=0))
               for i in ids]
        for i, (b, hp) in enumerate(streams):
            yacc_ref[b, rows, sls[hp]] = xs[i][C:] + av[i][C:] + yu[i]
        decay = [jnp.transpose(jnp.broadcast_to(d_end[i], (LANES, LANES))) for i in ids]
        return [(s[i] + jnp.where(same_head, upd[i], 0.0)) * decay[i] for i in ids]

    n_chunks = sb // C
    state = [s_ref[i] for i in ids]
    ops = prep(0)
    for c in range(n_chunks):
        hd = head(ops, state)
        nxt = prep(c + 1) if c + 1 < n_chunks else None
        state = tail(c, ops, hd, state)
        ops = nxt
    for i in ids:
        s_ref[i] = state[i]

    head_ones = jnp.where(same_head, 1.0, 0.0).astype(BF16)
    y = jnp.concatenate([yacc_ref[b, :, sls[hp]] for b, hp in streams], axis=0)
    hi = y.astype(BF16)
    lo = (y - hi.astype(F32)).astype(BF16)
    mean = (_dot_nn(hi, head_ones) + _dot_nn(lo, head_ones)) * (1.0 / RWKV_HEAD)
    rkk = jnp.concatenate([r_ref[b, :, sls[hp]] * k_ref[b, :, sls[hp]] * rk_ref[:, sls[hp]] for b, hp in streams],
                          axis=0)
    rks = _dot_nn(rkk.astype(BF16), head_ones)
    d = y - mean
    var = _dot_nn((d * d).astype(BF16), head_ones) * (1.0 / RWKV_HEAD)
    for i, (b, hp) in enumerate(streams):
        sl, rows = sls[hp], slice(i * sb, (i + 1) * sb)
        yn = d[rows] * lax.rsqrt(var[rows] + GN_EPS) * gnw_ref[:, sl] + gnb_ref[:, sl]
        y_o[b, :, sl] = ((yn + rks[rows] * v_ref[b, :, sl]) * g_ref[b, :, sl]).astype(BF16)


def _scan(r, k, v, na, bb, lw, g, gn_w, gn_b, r_k, *, batch, seq, sb, nbat):
    T, W = r.shape
    n_streams = nbat * (W // LANES)
    blk = lambda b, j: (b, j, 0)
    const = lambda b, j: (0, 0)
    in_specs = [pl.BlockSpec((nbat, sb, W), blk)] * 7 + [pl.BlockSpec((1, W), const)] * 3
    seq3 = lambda a: a.reshape(batch, seq, W)
    y = pl.pallas_call(
        functools.partial(_scan_kernel, sb=sb, nbat=nbat),
        grid=(batch // nbat, seq // sb),
        in_specs=in_specs,
        out_specs=pl.BlockSpec((nbat, sb, W), blk),
        out_shape=jax.ShapeDtypeStruct((batch, seq, W), BF16),
        scratch_shapes=[pltpu.VMEM((n_streams, LANES, LANES), F32), pltpu.VMEM((nbat, sb, W), F32)],
        compiler_params=pltpu.CompilerParams(
            dimension_semantics=("arbitrary", "arbitrary"), vmem_limit_bytes=VMEM_LIMIT),
        name="rwkv_scan",
    )(*(seq3(a) for a in (r, k, v, na, bb, lw, g)), gn_w, gn_b, r_k)
    return y.reshape(T, W)


def _tile(n, pref):
    return pref if n % pref == 0 else CHUNK


def kernel(x, ffn1_norm, ffn1_w1, ffn1_w3, ffn1_w2, mix_norm, w_in, mu_shift, w0, w_lora_up, a0, a_lora_up, g_lora_up, k_k, k_a, r_k, gn_w, gn_b, sgu_ln_g, sgu_ln_b, sgu_w, sgu_b, w_out, ffn2_norm, ffn2_w1, ffn2_w3, ffn2_w2, final_norm):
    B, S, D = x.shape
    T = B * S
    depth = ffn1_norm.shape[0]
    tm_ffn = _tile(S, 1024)
    tm_mix = _tile(S, 1024)
    sb = _tile(S, 256)
    rowv = lambda a: a.reshape(1, -1).astype(F32)

    xt = x.reshape(T, D)
    for l in range(depth):
        f1, f3, f2 = _narrow([ffn1_w1, ffn1_w3, ffn1_w2], l, n_steps=NARROW_STEPS)
        later = [w_in, w_out, ffn2_w1, ffn2_w3, ffn2_w2]
        xt, (w_in_b, w_out_b, w1_b, w3_b, w2_b) = _ffn(
            xt, rowv(ffn1_norm[l]), f1, f3, f2, tm=tm_ffn, narrow=later, layer=l)
        zeros = jnp.zeros((DECAY_LORA, RWKV_WIDTH), F32)
        lora = jnp.concatenate([jnp.concatenate([w_lora_up[l], zeros], axis=1),
                                jnp.concatenate([zeros, a_lora_up[l]], axis=1)], axis=0)
        sgu_bias = jnp.repeat(jnp.transpose(sgu_b[l]), GMLP_WIDTH // GMLP_GROUPS, axis=1)
        r, k, v, na, bb, lw, g, y_gmlp = _mixin(
            xt, rowv(mix_norm[l]), w_in_b, rowv(mu_shift[l]), rowv(w0[l]), rowv(a0[l]),
            lora.astype(BF16), g_lora_up[l].astype(BF16), rowv(k_k[l]), rowv(k_a[l]),
            rowv(sgu_ln_g[l]), rowv(sgu_ln_b[l]), sgu_w[l], sgu_bias, tm=tm_mix, seq=S)
        y_rwkv = _scan(r, k, v, na, bb, lw, g, rowv(gn_w[l]), rowv(gn_b[l]), rowv(r_k[l]),
                       batch=B, seq=S, sb=sb, nbat=SCAN_BATCH if B % SCAN_BATCH == 0 else 1)
        last = l == depth - 1
        xt, _ = _ffn(xt, rowv(ffn2_norm[l]), w1_b, w3_b, w2_b, tm=tm_ffn,
                     mix_in=(y_rwkv, y_gmlp, w_out_b),
                     final_g=rowv(final_norm) if last else None)
    if depth == 0:
        raise ValueError("depth must be positive")
    return xt.reshape(B, S, D)
```
